```python
import math
import jax, jax.numpy as jnp
from jax import lax
import numpy as np

D_MODEL = 1024
BATCH = 8
SEQ = 8192
DEPTH = 2
DEC_BATCH = 8
DEC_SEQ = 64
PAST_LEN = 1024

CHUNK = 64
RMS_EPS = 1e-6
A_HEADS = 4
A_DK = 128
A_DV = 128
A_WIDTH = A_HEADS * A_DV
A_COLS = 4 * A_WIDTH
LB_FLOOR = 1e-30
B_HEADS = 8
B_HEADDIM = 64
B_INNER = B_HEADS * B_HEADDIM
B_GROUPS = 2
B_HPG = B_HEADS // B_GROUPS
B_STATE = 128
B_CONV = 4
B_CONV_DIM = B_INNER + 2 * B_GROUPS * B_STATE
B_COLS = B_INNER + B_CONV_DIM + B_HEADS
C_HEADS = 8
C_HEADDIM = 64
C_WIDTH = C_HEADS * C_HEADDIM
C_DECAY_LORA = 64
C_ICLR_LORA = 64
C_GATE_LORA = 128
C_COLS = 3 * C_WIDTH + C_DECAY_LORA + C_ICLR_LORA + C_GATE_LORA
C_LN_EPS = 64e-5
N_BRANCH = 3
GATE_COLS = N_BRANCH * D_MODEL
IN_COLS = A_COLS + B_COLS + C_COLS + GATE_COLS
MIX_WIDTH = A_WIDTH + B_INNER + C_WIDTH
N_EXPERTS = 16
N_EXPERT_GROUPS = 4
EXPERTS_PER_GROUP = N_EXPERTS // N_EXPERT_GROUPS
TOP_K = 2
D_FF_EXPERT = 512
MOE_BLOCK = 256

kernel_name = 'hybrid_hgrn2_ssd_rwkv7_grouped_moe_stream_step'


def rmsnorm(x, g):
    xf = x.astype(jnp.float32)
    y = xf * lax.rsqrt(jnp.mean(xf * xf, axis=-1, keepdims=True) + RMS_EPS)
    return (y * g.astype(jnp.float32)).astype(x.dtype)


def to_blocks(t, L):
    b, T = t.shape[0], t.shape[1]
    return jnp.moveaxis(t.reshape(b, T // L, L, *t.shape[2:]), 1, 0)


def from_blocks(t):
    t = jnp.moveaxis(t, 0, 1)
    return t.reshape(t.shape[0], t.shape[1] * t.shape[2], *t.shape[3:])


def masked_decay(causal, diff):
    return jnp.where(causal, jnp.exp(jnp.where(causal, diff, 0.0)), 0.0)


def hgrn2_mixer(cols, lower, norm_g, state0):
    f32 = jnp.float32
    Bsz, T, _ = cols.shape
    L = min(CHUNK, T)
    q, f_pre, i_val, g = jnp.split(cols.astype(f32), 4, axis=-1)
    lower = lower.astype(f32)
    log_f = jnp.logaddexp(jnp.log(jnp.maximum(lower, LB_FLOOR)), jnp.log1p(-lower) + jax.nn.log_sigmoid(f_pre))
    k = (1.0 - lower) * jax.nn.sigmoid(-f_pre)
    hd = lambda t: to_blocks(t.reshape(Bsz, T, A_HEADS, -1), L)
    causal = jnp.tril(jnp.ones((L, L), dtype=bool))[None, :, :, None, None]

    def step(S, inp):
        qb, kb, vb, lfb = inp
        b = jnp.cumsum(lfb, axis=1)
        o = jnp.einsum('bthk,bhkv->bthv', qb * jnp.exp(b), S)
        decay = masked_decay(causal, b[:, :, None] - b[:, None])
        att = jnp.einsum('bthk,bshk,btshk->btsh', qb, kb, decay)
        o = o + jnp.einsum('btsh,bshv->bthv', att, vb)
        b_last = b[:, -1]
        S = jnp.exp(b_last)[..., None] * S + jnp.einsum('bshk,bshv->bhkv', kb * jnp.exp(b_last[:, None] - b), vb)
        return S, o

    S_fin, o = lax.scan(step, state0.astype(f32), (hd(q), hd(k), hd(i_val), hd(log_f)))
    o = from_blocks(o)
    o = rmsnorm(o, norm_g.reshape(A_HEADS, A_DV)).reshape(Bsz, T, A_WIDTH) * jax.nn.silu(g)
    return o.astype(cols.dtype), S_fin


def ssd_mixer(cols, conv_w, conv_b, dt_bias, a_log, d_skip, norm_g, conv_state, ssm_state):
    f32 = jnp.float32
    Bsz, T, _ = cols.shape
    L = min(CHUNK, T)
    z = cols[..., :B_INNER]
    xbc = cols[..., B_INNER:B_INNER + B_CONV_DIM]
    dt_raw = cols[..., B_INNER + B_CONV_DIM:]
    padded = jnp.concatenate([conv_state.astype(cols.dtype), xbc], axis=1)
    conv = lax.conv_general_dilated(padded, conv_w[:, None, :].astype(cols.dtype), (1,), 'VALID',
                                    dimension_numbers=('NWC', 'WIO', 'NWC'),
                                    feature_group_count=B_CONV_DIM)
    new_conv = padded[:, T:]
    xbc = jax.nn.silu((conv + conv_b).astype(f32))
    xs = xbc[..., :B_INNER].reshape(Bsz, T, B_GROUPS, B_HPG, B_HEADDIM)
    Bm = xbc[..., B_INNER:B_INNER + B_GROUPS * B_STATE].reshape(Bsz, T, B_GROUPS, B_STATE)
    Cm = xbc[..., B_INNER + B_GROUPS * B_STATE:].reshape(Bsz, T, B_GROUPS, B_STATE)
    dt = jax.nn.softplus(dt_raw.astype(f32) + dt_bias.astype(f32)).reshape(Bsz, T, B_GROUPS, B_HPG)
    dA = dt * (-jnp.exp(a_log.astype(f32))).reshape(B_GROUPS, B_HPG)
    causal = jnp.tril(jnp.ones((L, L), dtype=bool))[None, :, :, None, None]

    def step(S, inp):
        xb, bb, cb, dtb, dab = inp
        a = jnp.cumsum(dab, axis=1)
        seg = masked_decay(causal, a[:, :, None] - a[:, None])
        m = jnp.einsum('btgn,bsgn->btsg', cb, bb)[..., None] * seg * dtb[:, None]
        y = jnp.einsum('btsgr,bsgrp->btgrp', m, xb)
        y = y + jnp.einsum('btgn,bgrpn->btgrp', cb, S) * jnp.exp(a)[..., None]
        a_last = a[:, -1]
        S = jnp.exp(a_last)[..., None, None] * S + jnp.einsum(
            'bsgr,bsgrp,bsgn->bgrpn', jnp.exp(a_last[:, None] - a) * dtb, xb, bb)
        return S, y

    S0 = ssm_state.astype(f32).reshape(Bsz, B_GROUPS, B_HPG, B_HEADDIM, B_STATE)
    S_fin, y = lax.scan(step, S0, (to_blocks(xs, L), to_blocks(Bm, L), to_blocks(Cm, L),
                                   to_blocks(dt, L), to_blocks(dA, L)))
    y = from_blocks(y) + d_skip.astype(f32).reshape(B_GROUPS, B_HPG, 1) * xs
    y = rmsnorm(y.reshape(Bsz, T, B_INNER) * jax.nn.silu(z.astype(f32)), norm_g)
    return y.astype(cols.dtype), new_conv, S_fin.reshape(Bsz, B_HEADS, B_HEADDIM, B_STATE)


def rwkv7_mixer(cols, shift_state, wkv_state, mu, w0, w_w2, a0, a_w2, g_w2, k_k, k_a, r_k, ln_w, ln_b):
    f32 = jnp.float32
    Bsz, T, _ = cols.shape
    prev = jnp.concatenate([shift_state.astype(cols.dtype), cols[:, :-1]], axis=1)
    new_shift = cols[:, T - 1:]
    mixed = (cols + (prev - cols) * mu).astype(f32)
    s3 = 3 * C_WIDTH
    r, k, v, w_lo, a_lo, g_lo = jnp.split(
        mixed, [C_WIDTH, 2 * C_WIDTH, s3, s3 + C_DECAY_LORA, s3 + C_DECAY_LORA + C_ICLR_LORA], axis=-1)
    w = -jax.nn.softplus(-(w0 + jnp.tanh(w_lo) @ w_w2)) - 0.5
    decay = jnp.exp(-jnp.exp(w))
    a = jax.nn.sigmoid(a0 + a_lo @ a_w2)
    g = jax.nn.sigmoid(g_lo) @ g_w2
    hd = lambda t: t.reshape(Bsz, T, C_HEADS, C_HEADDIM)
    kk = hd(k * k_k)
    kk = kk / jnp.maximum(jnp.sqrt(jnp.sum(kk * kk, axis=-1, keepdims=True)), 1e-12)
    k = k * (1.0 + (a - 1.0) * k_a)
    rh, kh, vh, ah, wh = hd(r), hd(k), hd(v), hd(a), hd(decay)
    tm = lambda t: jnp.moveaxis(t, 1, 0)

    def step(S, inp):
        rt, wt, kt, vt, kkt, at = inp
        sa = jnp.einsum('bhvk,bhk->bhv', S, -kkt)
        S = S * wt[:, :, None, :] + sa[..., None] * (kkt * at)[:, :, None, :] + vt[..., None] * kt[:, :, None, :]
        return S, jnp.einsum('bhvk,bhk->bhv', S, rt)

    S_fin, y = lax.scan(step, wkv_state.astype(f32), (tm(rh), tm(wh), tm(kh), tm(vh), tm(kk), tm(ah)))
    y = jnp.moveaxis(y, 0, 1)
    mean = jnp.mean(y, axis=-1, keepdims=True)
    var = jnp.mean(jnp.square(y - mean), axis=-1, keepdims=True)
    y = ((y - mean) * lax.rsqrt(var + C_LN_EPS)).reshape(Bsz, T, C_WIDTH) * ln_w + ln_b
    bonus = jnp.sum(rh * kh * r_k, axis=-1, keepdims=True) * vh
    y = (y + bonus.reshape(Bsz, T, C_WIDTH)) * g
    return y.astype(cols.dtype), new_shift, S_fin


def mixer_block(h, l, st_hgrn, st_ssm, st_conv, st_wkv, st_shift, P):
    Bsz, T, _ = h.shape
    cols = h @ P['w_in'][l]
    c_a = cols[..., :A_COLS]
    c_b = cols[..., A_COLS:A_COLS + B_COLS]
    c_c = cols[..., A_COLS + B_COLS:A_COLS + B_COLS + C_COLS]
    c_g = cols[..., A_COLS + B_COLS + C_COLS:]
    o_a, hgrn_new = hgrn2_mixer(c_a, P['hgrn_lower'][l], P['hgrn_norm_g'][l], st_hgrn)
    o_b, conv_new, ssm_new = ssd_mixer(c_b, P['ssm_conv_w'][l], P['ssm_conv_b'][l], P['ssm_dt_bias'][l],
                                       P['ssm_a_log'][l], P['ssm_d'][l], P['ssm_norm_g'][l], st_conv, st_ssm)
    o_c, shift_new, wkv_new = rwkv7_mixer(c_c, st_shift, st_wkv, P['rwkv_mu'][l], P['rwkv_w0'][l],
                                          P['rwkv_w_w2'][l], P['rwkv_a0'][l], P['rwkv_a_w2'][l],
                                          P['rwkv_g_w2'][l], P['rwkv_k_k'][l], P['rwkv_k_a'][l],
                                          P['rwkv_r_k'][l], P['rwkv_ln_w'][l], P['rwkv_ln_b'][l])
    gates = jax.nn.sigmoid(c_g.reshape(Bsz, T, N_BRANCH, D_MODEL))
    wb = P['w_branch'][l]
    merged = (gates[:, :, 0] * (o_a @ wb[:A_WIDTH])
              + gates[:, :, 1] * (o_b @ wb[A_WIDTH:A_WIDTH + B_INNER])
              + gates[:, :, 2] * (o_c @ wb[A_WIDTH + B_INNER:]))
    return merged @ P['w_out'][l], (hgrn_new, ssm_new, conv_new, wkv_new, shift_new)


def route(hf, w_router, b_router):
    f32 = jnp.float32
    s = jax.nn.sigmoid(jnp.dot(hf.astype(f32), w_router.astype(f32)))
    sel = (s + b_router.astype(f32)).reshape(-1, N_EXPERT_GROUPS, EXPERTS_PER_GROUP)
    g_score = jnp.sum(lax.top_k(sel, TOP_K)[0], axis=-1)
    g_idx = jnp.argmax(g_score, axis=-1).astype(jnp.int32)
    sel_in = jnp.take_along_axis(sel, g_idx[:, None, None], axis=1)[:, 0]
    local = lax.top_k(sel_in, TOP_K)[1].astype(jnp.int32)
    idx = g_idx[:, None] * EXPERTS_PER_GROUP + local
    w = jnp.take_along_axis(s, idx, axis=-1)
    return idx, w / jnp.sum(w, axis=-1, keepdims=True)


def moe_experts(hf, idx, wsel, w_gate, w_up, w_down):
    n_tok, d = hf.shape
    n_slot = n_tok * TOP_K
    flat_e = idx.reshape(n_slot)
    order = jnp.argsort(flat_e).astype(jnp.int32)
    sorted_e = flat_e[order]
    counts = jnp.zeros((N_EXPERTS,), jnp.int32).at[flat_e].add(1)
    padded = (counts + MOE_BLOCK - 1) // MOE_BLOCK * MOE_BLOCK
    start = jnp.cumsum(counts) - counts
    pad_end = jnp.cumsum(padded)
    pad_start = pad_end - padded
    dest = pad_start[sorted_e] + jnp.arange(n_slot, dtype=jnp.int32) - start[sorted_e]
    n_blocks = -(-n_slot // MOE_BLOCK) + N_EXPERTS
    tok = order // TOP_K
    row_tok = jnp.zeros((n_blocks * MOE_BLOCK,), jnp.int32).at[dest].set(tok)
    block_e = jnp.minimum(jnp.searchsorted(pad_end, jnp.arange(n_blocks, dtype=jnp.int32) * MOE_BLOCK,
                                           side='right'), N_EXPERTS - 1).astype(jnp.int32)

    def expert_block(args):
        rows, e = args
        xb = hf[rows]
        hid = jax.nn.silu(xb @ w_gate[e]) * (xb @ w_up[e])
        return hid @ w_down[e]

    yb = lax.map(expert_block, (row_tok.reshape(n_blocks, MOE_BLOCK), block_e))
    y_slot = yb.reshape(n_blocks * MOE_BLOCK, d)[dest]
    coef = wsel.reshape(n_slot)[order].astype(y_slot.dtype)
    return jnp.zeros_like(hf).at[tok].add((y_slot * coef[:, None]).astype(hf.dtype))


def moe_layer(h, w_router, b_router, w_gate, w_up, w_down):
    Bsz, T, D = h.shape
    hf = h.reshape(Bsz * T, D)
    idx, wsel = route(hf, w_router, b_router)
    return moe_experts(hf, idx, wsel, w_gate, w_up, w_down).reshape(Bsz, T, D)


def trunk(x, c, st_hgrn, st_ssm, st_conv, st_wkv, st_shift, P):
    ada = jnp.einsum('bd,lde->lbe', jax.nn.silu(c), P['w_ada']) + P['b_ada'][:, None, :]
    outs = []
    for l in range(DEPTH):
        shift1, scale1, gate1, shift2, scale2, gate2 = jnp.split(ada[l][:, None, :], 6, axis=-1)
        h = rmsnorm(x, P['norm1_g'][l]) * (1.0 + scale1) + shift1
        mix, st = mixer_block(h, l, st_hgrn[l], st_ssm[l], st_conv[l], st_wkv[l], st_shift[l], P)
        x = x + gate1 * mix
        h = rmsnorm(x, P['norm2_g'][l]) * (1.0 + scale2) + shift2
        x = x + gate2 * moe_layer(h, P['w_router'], P['b_router'], P['w_gate_e'][l], P['w_up_e'][l], P['w_down_e'][l])
        outs.append(st)
    new_states = [jnp.stack(s, axis=0) for s in zip(*outs)]
    return rmsnorm(x, P['final_g']), new_states


def setup_inputs(seed: int = 0) -> dict:
    key = jax.random.key(seed)
    ks = iter(jax.random.split(key, 48))
    f32 = jnp.float32
    nrm = lambda shape, scale: scale * jax.random.normal(next(ks), shape, f32)
    uni = lambda shape, lo, hi: jax.random.uniform(next(ks), shape, f32, lo, hi)
    dt0 = jnp.exp(uni((DEPTH, B_HEADS), math.log(1e-3), math.log(1e-1)))
    return {
        'x_prompt': nrm((BATCH, SEQ, D_MODEL), 1.0),
        'x_sample': nrm((DEC_BATCH, DEC_SEQ, D_MODEL), 1.0),
        'c_prompt': nrm((BATCH, D_MODEL), 1.0),
        'c_sample': nrm((DEC_BATCH, D_MODEL), 1.0),
        'state_hgrn': nrm((DEPTH, DEC_BATCH, A_HEADS, A_DK, A_DV), 0.5),
        'state_ssm': nrm((DEPTH, DEC_BATCH, B_HEADS, B_HEADDIM, B_STATE), 0.1),
        'state_conv': nrm((DEPTH, DEC_BATCH, B_CONV - 1, B_CONV_DIM), 1.0),
        'state_wkv': nrm((DEPTH, DEC_BATCH, C_HEADS, C_HEADDIM, C_HEADDIM), 0.5),
        'state_shift': nrm((DEPTH, DEC_BATCH, 1, C_COLS), 1.0),
        'w_ada': nrm((DEPTH, D_MODEL, 6 * D_MODEL), 0.5 * D_MODEL ** -0.5),
        'b_ada': nrm((DEPTH, 6 * D_MODEL), 0.01),
        'norm1_g': 1.0 + nrm((DEPTH, D_MODEL), 0.01),
        'norm2_g': 1.0 + nrm((DEPTH, D_MODEL), 0.01),
        'final_g': 1.0 + nrm((D_MODEL,), 0.01),
        'w_in': nrm((DEPTH, D_MODEL, IN_COLS), D_MODEL ** -0.5),
        'hgrn_lb': nrm((DEPTH, A_WIDTH), 0.5),
        'hgrn_norm_g': 1.0 + nrm((DEPTH, A_WIDTH), 0.01),
        'ssm_conv_w': nrm((DEPTH, B_CONV, B_CONV_DIM), B_CONV ** -0.5),
        'ssm_conv_b': nrm((DEPTH, B_CONV_DIM), 0.01),
        'ssm_dt_bias': dt0 + jnp.log(-jnp.expm1(-dt0)),
        'ssm_a_log': jnp.log(uni((DEPTH, B_HEADS), 1.0, 16.0)),
        'ssm_d': 1.0 + nrm((DEPTH, B_HEADS), 0.1),
        'ssm_norm_g': 1.0 + nrm((DEPTH, B_INNER), 0.01),
        'rwkv_mu': uni((DEPTH, C_COLS), 0.0, 1.0),
        'rwkv_w0': uni((DEPTH, C_WIDTH), -2.0, 1.0),
        'rwkv_w_w2': nrm((DEPTH, C_DECAY_LORA, C_WIDTH), 0.1),
        'rwkv_a0': nrm((DEPTH, C_WIDTH), 0.1),
        'rwkv_a_w2': nrm((DEPTH, C_ICLR_LORA, C_WIDTH), 0.1),
        'rwkv_g_w2': nrm((DEPTH, C_GATE_LORA, C_WIDTH), C_GATE_LORA ** -0.5),
        'rwkv_k_k': 0.85 + nrm((DEPTH, C_WIDTH), 0.02),
        'rwkv_k_a': 1.0 + nrm((DEPTH, C_WIDTH), 0.02),
        'rwkv_r_k': nrm((DEPTH, C_HEADS, C_HEADDIM), 0.1),
        'rwkv_ln_w': 1.0 + nrm((DEPTH, C_WIDTH), 0.01),
        'rwkv_ln_b': nrm((DEPTH, C_WIDTH), 0.01),
        'w_branch': nrm((DEPTH, MIX_WIDTH, D_MODEL), A_WIDTH ** -0.5),
        'w_out': nrm((DEPTH, D_MODEL, D_MODEL), D_MODEL ** -0.5),
        'w_router': nrm((D_MODEL, N_EXPERTS), D_MODEL ** -0.5),
        'b_router': nrm((N_EXPERTS,), 0.01),
        'w_gate_e': nrm((DEPTH, N_EXPERTS, D_MODEL, D_FF_EXPERT), D_MODEL ** -0.5),
        'w_up_e': nrm((DEPTH, N_EXPERTS, D_MODEL, D_FF_EXPERT), D_MODEL ** -0.5),
        'w_down_e': nrm((DEPTH, N_EXPERTS, D_FF_EXPERT, D_MODEL), D_FF_EXPERT ** -0.5),
    }


def reference(x_prompt, x_sample, c_prompt, c_sample, state_hgrn, state_ssm, state_conv, state_wkv, state_shift,
              w_ada, b_ada, norm1_g, norm2_g, final_g, w_in, hgrn_lb, hgrn_norm_g,
              ssm_conv_w, ssm_conv_b, ssm_dt_bias, ssm_a_log, ssm_d, ssm_norm_g,
              rwkv_mu, rwkv_w0, rwkv_w_w2, rwkv_a0, rwkv_a_w2, rwkv_g_w2, rwkv_k_k, rwkv_k_a, rwkv_r_k,
              rwkv_ln_w, rwkv_ln_b, w_branch, w_out, w_router, b_router, w_gate_e, w_up_e, w_down_e):
    p_lb = jax.nn.softmax(hgrn_lb.astype(jnp.float32), axis=0)
    hgrn_lower = jnp.cumsum(p_lb, axis=0) - p_lb[0:1]
    P = {'w_ada': w_ada, 'b_ada': b_ada, 'norm1_g': norm1_g, 'norm2_g': norm2_g, 'final_g': final_g,
         'w_in': w_in, 'hgrn_lower': hgrn_lower, 'hgrn_norm_g': hgrn_norm_g,
         'ssm_conv_w': ssm_conv_w, 'ssm_conv_b': ssm_conv_b, 'ssm_dt_bias': ssm_dt_bias,
         'ssm_a_log': ssm_a_log, 'ssm_d': ssm_d, 'ssm_norm_g': ssm_norm_g,
         'rwkv_mu': rwkv_mu, 'rwkv_w0': rwkv_w0, 'rwkv_w_w2': rwkv_w_w2, 'rwkv_a0': rwkv_a0,
         'rwkv_a_w2': rwkv_a_w2, 'rwkv_g_w2': rwkv_g_w2, 'rwkv_k_k': rwkv_k_k, 'rwkv_k_a': rwkv_k_a,
         'rwkv_r_k': rwkv_r_k, 'rwkv_ln_w': rwkv_ln_w, 'rwkv_ln_b': rwkv_ln_b,
         'w_branch': w_branch, 'w_out': w_out, 'w_router': w_router, 'b_router': b_router,
         'w_gate_e': w_gate_e, 'w_up_e': w_up_e, 'w_down_e': w_down_e}
    bp = x_prompt.shape[0]
    z_hgrn = jnp.zeros((DEPTH, bp, A_HEADS, A_DK, A_DV), jnp.float32)
    z_ssm = jnp.zeros((DEPTH, bp, B_HEADS, B_HEADDIM, B_STATE), jnp.float32)
    z_conv = jnp.zeros((DEPTH, bp, B_CONV - 1, B_CONV_DIM), x_prompt.dtype)
    z_wkv = jnp.zeros((DEPTH, bp, C_HEADS, C_HEADDIM, C_HEADDIM), jnp.float32)
    z_shift = jnp.zeros((DEPTH, bp, 1, C_COLS), x_prompt.dtype)
    y_prompt, st_p = trunk(x_prompt, c_prompt, z_hgrn, z_ssm, z_conv, z_wkv, z_shift, P)
    y_sample, st_s = trunk(x_sample, c_sample, state_hgrn, state_ssm, state_conv, state_wkv, state_shift, P)
    hgrn_p, ssm_p, conv_p, wkv_p, shift_p = st_p
    hgrn_s, ssm_s, conv_s, wkv_s, shift_s = st_s
    return (y_prompt, y_sample, hgrn_p, ssm_p, conv_p, wkv_p, shift_p, hgrn_s, ssm_s, conv_s, wkv_s, shift_s)
```

```python
import functools

import jax
import jax.numpy as jnp
from jax import lax
from jax.experimental import pallas as pl
from jax.experimental.pallas import tpu as pltpu

F32 = jnp.float32
BF16 = jnp.bfloat16

D_MODEL = 1024
RMS_EPS = 1e-6
A_HEADS = 4
A_DK = 128
A_DV = 128
A_WIDTH = A_HEADS * A_DV
A_COLS = 4 * A_WIDTH
LB_FLOOR = 1e-30
B_HEADS = 8
B_HEADDIM = 64
B_INNER = B_HEADS * B_HEADDIM
B_GROUPS = 2
B_HPG = B_HEADS // B_GROUPS
B_STATE = 128
B_CONV = 4
B_CONV_DIM = B_INNER + 2 * B_GROUPS * B_STATE
B_COLS = B_INNER + B_CONV_DIM + B_HEADS
C_HEADS = 8
C_HEADDIM = 64
C_WIDTH = C_HEADS * C_HEADDIM
C_DECAY_LORA = 64
C_ICLR_LORA = 64
C_GATE_LORA = 128
C_COLS = 3 * C_WIDTH + C_DECAY_LORA + C_ICLR_LORA + C_GATE_LORA
C_LN_EPS = 64e-5
N_BRANCH = 3
GATE_COLS = N_BRANCH * D_MODEL
MIX_WIDTH = A_WIDTH + B_INNER + C_WIDTH
N_EXPERTS = 16
N_EXPERT_GROUPS = 4
EXPERTS_PER_GROUP = N_EXPERTS // N_EXPERT_GROUPS
D_FF_EXPERT = 512

LANES = 128
SUBLANES = 8
VMEM_LIMIT = 56 * 1024 * 1024

B_DT_PAD = LANES
B_COLS_PAD = B_INNER + B_CONV_DIM + B_DT_PAD
C_LORA_PAD = LANES
C_COLS_PAD = 3 * C_WIDTH + 3 * C_LORA_PAD

TIME_BLOCK = 64
HGRN_SUB = 32


def _nn(a, b):
    return lax.dot_general(a, b, (((1,), (0,)), ((), ())), preferred_element_type=F32)


def _nt(a, b):
    return lax.dot_general(a, b, (((1,), (1,)), ((), ())), preferred_element_type=F32)


def _tn(a, b):
    return lax.dot_general(a, b, (((0,), (0,)), ((), ())), preferred_element_type=F32)


def _split2(x):
    hi = x.astype(BF16)
    lo = (x - hi.astype(F32)).astype(BF16)
    return hi, lo


def _split3(x):
    hi = x.astype(BF16)
    r1 = x - hi.astype(F32)
    mid = r1.astype(BF16)
    lo = (r1 - mid.astype(F32)).astype(BF16)
    return hi, mid, lo


def _mm(dot, a, b, passes):
    if passes == 1:
        return dot(a.astype(BF16), b.astype(BF16))
    a_hi, a_lo = _split2(a)
    b_hi, b_lo = _split2(b)
    return dot(a_hi, b_hi) + (dot(a_hi, b_lo) + dot(a_lo, b_hi))


def _exact_lhs_nn(m_bf16, x):
    hi, mid, lo = _split3(x)
    return _nn(m_bf16, hi) + (_nn(m_bf16, mid) + _nn(m_bf16, lo))


def _exact_rhs_nn(x, m_bf16):
    hi, mid, lo = _split3(x)
    return _nn(hi, m_bf16) + (_nn(mid, m_bf16) + _nn(lo, m_bf16))


def _sigmoid(x):
    return 1.0 / (1.0 + jnp.exp(-x))


def _silu(x):
    return x * _sigmoid(x)


def _softplus(x):
    return jnp.maximum(x, 0.0) + jnp.log1p(jnp.exp(-jnp.abs(x)))


def _iota2(shape, dim):
    return lax.broadcasted_iota(jnp.int32, shape, dim)


def _tril_bf16(n):
    return jnp.where(_iota2((n, n), 1) <= _iota2((n, n), 0), 1.0, 0.0).astype(BF16)


def _norm_mod(x, g, shift, scale, seqs):
    rows, d = x.shape
    y = x * lax.rsqrt(jnp.mean(x * x, axis=-1, keepdims=True) + RMS_EPS) * g
    y = y.reshape(seqs, rows // seqs, d) * (1.0 + scale) + shift
    return y.reshape(rows, d)


def _tile_plan(t_len, tile):
    if tile >= t_len:
        assert tile % t_len == 0
        return tile // t_len, 1
    assert t_len % tile == 0
    return 1, t_len // tile


def _params(*sem):
    return pltpu.CompilerParams(dimension_semantics=sem, vmem_limit_bytes=VMEM_LIMIT)


def _ada_kernel(c_ref, w_ref, b_ref, o_ref):
    s = _silu(c_ref[...])
    o_ref[...] = _mm(_nn, s, w_ref[...], 1) + b_ref[...]


def _ada(c, w_ada, b_ada):
    nb = c.shape[0]
    depth = w_ada.shape[0]
    n_blk = w_ada.shape[2] // D_MODEL
    return pl.pallas_call(
        _ada_kernel,
        grid=(depth, n_blk),
        in_specs=[pl.BlockSpec((nb, D_MODEL), lambda l, j: (0, 0)),
                  pl.BlockSpec((None, D_MODEL, D_MODEL), lambda l, j: (l, 0, j)),
                  pl.BlockSpec((None, 1, D_MODEL), lambda l, j: (l, 0, j))],
        out_specs=pl.BlockSpec((None, nb, D_MODEL), lambda l, j: (l, 0, j)),
        out_shape=jax.ShapeDtypeStruct((depth, nb, w_ada.shape[2]), F32),
        compiler_params=_params("arbitrary", "arbitrary"),
        name="ada",
    )(c, w_ada, b_ada.reshape(depth, 1, -1))


def _inproj_kernel(seqs, x_ref, g_ref, sh_ref, sc_ref, wa_ref, wb_ref, wc_ref, oa_ref, ob_ref, oc_ref):
    h = _norm_mod(x_ref[...], g_ref[...], sh_ref[...], sc_ref[...], seqs).astype(BF16)
    oa_ref[...] = _nn(h, wa_ref[...])
    ob_ref[...] = _nn(h, wb_ref[...])
    oc_ref[...] = _nn(h, wc_ref[...])


def _inproj(x, t_len, g, ada3, wa, wb, wc, tile):
    n = x.shape[0]
    seqs, tps = _tile_plan(t_len, tile)
    full = lambda a: pl.BlockSpec(a.shape, lambda i: (0,) * a.ndim)
    ada_spec = lambda j: pl.BlockSpec((seqs, 1, D_MODEL), lambda i: (i // tps, 0, j))
    row = lambda w: pl.BlockSpec((tile, w), lambda i: (i, 0))
    return pl.pallas_call(
        functools.partial(_inproj_kernel, seqs),
        grid=(n // tile,),
        in_specs=[row(D_MODEL), full(g), ada_spec(0), ada_spec(1), full(wa), full(wb), full(wc)],
        out_specs=[row(A_COLS), row(B_COLS_PAD), row(C_COLS_PAD)],
        out_shape=[jax.ShapeDtypeStruct((n, A_COLS), F32),
                   jax.ShapeDtypeStruct((n, B_COLS_PAD), F32),
                   jax.ShapeDtypeStruct((n, C_COLS_PAD), F32)],
        compiler_params=_params("arbitrary"),
        name="inproj",
    )(x, g, ada3, ada3, wa, wb, wc)


def _hgrn_kernel(layer, cols_ref, lb_ref, ng_ref, s0_ref, o_ref, sout_ref, st_ref, oacc_ref):
    t = pl.program_id(1)
    tb = cols_ref.shape[0]

    @pl.when(t == 0)
    def _():
        for h in range(A_HEADS):
            st_ref[h] = s0_ref[0, h].T

    rows = [lb_ref[i:i + 1, :] for i in range(lb_ref.shape[0])]
    mx = functools.reduce(jnp.maximum, rows)
    es = [jnp.exp(r - mx) for r in rows]
    den = functools.reduce(lambda a, b: a + b, es)
    lower = jnp.zeros_like(mx)
    for i in range(1, layer + 1):
        lower = lower + es[i] / den

    q = cols_ref[:, 0:A_WIDTH]
    f_pre = cols_ref[:, A_WIDTH:2 * A_WIDTH]
    i_val = cols_ref[:, 2 * A_WIDTH:3 * A_WIDTH]
    g = cols_ref[:, 3 * A_WIDTH:4 * A_WIDTH]

    log_lower = jnp.log(jnp.maximum(lower, LB_FLOOR))
    log_sig = jnp.minimum(f_pre, 0.0) - jnp.log1p(jnp.exp(-jnp.abs(f_pre)))
    bterm = jnp.log1p(-lower) + log_sig
    log_f = jnp.maximum(log_lower, bterm) + jnp.log1p(jnp.exp(-jnp.abs(log_lower - bterm)))
    kf = (1.0 - lower) / (1.0 + jnp.exp(f_pre))

    sub = HGRN_SUB
    tril = _tril_bf16(sub)
    for c in range(tb // sub):
        r0 = c * sub
        b_all = _exact_lhs_nn(tril, log_f[r0:r0 + sub])
        for h in range(A_HEADS):
            hs = slice(h * A_DK, (h + 1) * A_DK)
            qh = q[r0:r0 + sub, hs]
            kh = kf[r0:r0 + sub, hs]
            vh = i_val[r0:r0 + sub, hs]
            bh = b_all[:, hs]
            st = st_ref[h]
            o_inter = _nt((qh * jnp.exp(bh)).astype(BF16), st.astype(BF16))
            for tt in range(sub):
                nr = (tt // SUBLANES + 1) * SUBLANES
                live = _iota2((nr, 1), 0) <= tt
                diff = bh[tt:tt + 1, :] - bh[0:nr]
                dec = jnp.where(live, jnp.exp(jnp.where(live, diff, 0.0)), 0.0)
                att = jnp.sum(qh[tt:tt + 1, :] * kh[0:nr] * dec, axis=-1, keepdims=True)
                oacc_ref[r0 + tt:r0 + tt + 1, hs] = jnp.sum(att * vh[0:nr], axis=0, keepdims=True)
            oacc_ref[r0:r0 + sub, hs] += o_inter
            b_last = bh[sub - 1:sub, :]
            kdec = kh * jnp.exp(b_last - bh)
            st_ref[h] = st * jnp.exp(b_last) + _tn(vh.astype(BF16), kdec.astype(BF16))

    for h in range(A_HEADS):
        hs = slice(h * A_DV, (h + 1) * A_DV)
        oh = oacc_ref[:, hs]
        on = oh * lax.rsqrt(jnp.mean(oh * oh, axis=-1, keepdims=True) + RMS_EPS) * ng_ref[:, hs]
        o_ref[:, hs] = (on * _silu(g[:, hs])).astype(o_ref.dtype)

    @pl.when(t == pl.num_programs(1) - 1)
    def _():
        for h in range(A_HEADS):
            sout_ref[0, h] = st_ref[h].T


def _hgrn(cols_a, nb, t_len, layer, hgrn_lb, norm_g, state0):
    tb = min(TIME_BLOCK, t_len)
    nt = t_len // tb
    return pl.pallas_call(
        functools.partial(_hgrn_kernel, layer),
        grid=(nb, nt),
        in_specs=[pl.BlockSpec((tb, A_COLS), lambda b, t: (b * nt + t, 0)),
                  pl.BlockSpec(hgrn_lb.shape, lambda b, t: (0, 0)),
                  pl.BlockSpec((1, A_WIDTH), lambda b, t: (0, 0)),
                  pl.BlockSpec((1, A_HEADS, A_DK, A_DV), lambda b, t: (b, 0, 0, 0))],
        out_specs=[pl.BlockSpec((tb, A_WIDTH), lambda b, t: (b * nt + t, 0)),
                   pl.BlockSpec((1, A_HEADS, A_DK, A_DV), lambda b, t: (b, 0, 0, 0))],
        out_shape=[jax.ShapeDtypeStruct((nb * t_len, A_WIDTH), BF16),
                   jax.ShapeDtypeStruct((nb, A_HEADS, A_DK, A_DV), F32)],
        scratch_shapes=[pltpu.VMEM((A_HEADS, A_DV, A_DK), F32),
                        pltpu.VMEM((tb, A_WIDTH), F32)],
        compiler_params=_params("arbitrary", "arbitrary"),
        name="hgrn2",
    )(cols_a, hgrn_lb, norm_g, state0)


def _ssd_kernel(cols_ref, cw_ref, cb_ref, dtb_ref, alog_ref, dsk_ref, ng_ref, cs0_ref, s0_ref,
                o_ref, sout_ref, ext_ref, st_ref):
    t = pl.program_id(1)
    tb = cols_ref.shape[0]
    n_pair = B_HEADS // 2
    pair_w = 2 * B_HEADDIM

    @pl.when(t == 0)
    def _():
        ext_ref[0:SUBLANES, :] = jnp.zeros((SUBLANES, B_CONV_DIM), F32)
        ext_ref[SUBLANES - (B_CONV - 1):SUBLANES, :] = cs0_ref[0]
        st_ref[...] = s0_ref[0]

    z = cols_ref[:, 0:B_INNER]
    ext_ref[SUBLANES:SUBLANES + tb, :] = cols_ref[:, B_INNER:B_INNER + B_CONV_DIM]
    conv = cb_ref[...]
    for j in range(B_CONV):
        conv = conv + cw_ref[j:j + 1, :] * ext_ref[pl.ds(SUBLANES - (B_CONV - 1) + j, tb), :]
    ext_ref[0:SUBLANES, :] = ext_ref[tb:tb + SUBLANES, :]
    xbc = _silu(conv)
    xs = xbc[:, 0:B_INNER]
    bm = xbc[:, B_INNER:B_INNER + B_GROUPS * B_STATE]
    cm = xbc[:, B_INNER + B_GROUPS * B_STATE:]

    dt = _softplus(cols_ref[:, B_INNER + B_CONV_DIM:] + dtb_ref[...])
    d_a = dt * (-jnp.exp(alog_ref[...]))
    a = _exact_lhs_nn(_tril_bf16(tb), d_a)
    a_t = a.T
    dt_t = dt.T
    causal = _iota2((tb, tb), 1) <= _iota2((tb, tb), 0)
    lane_lo = _iota2((1, pair_w), 1) < B_HEADDIM
    row_lo = _iota2((pair_w, 1), 0) < B_HEADDIM

    y_pairs = []
    for p in range(n_pair):
        grp = (2 * p) // B_HPG
        gs = slice(grp * B_STATE, (grp + 1) * B_STATE)
        bg = bm[:, gs]
        cg = cm[:, gs]
        cb = _nt(cg.astype(BF16), bg.astype(BF16))
        xp = xs[:, p * pair_w:(p + 1) * pair_w]
        sp = st_ref[p]
        y_inter = _nt(cg.astype(BF16), sp.astype(BF16))
        y_p = jnp.zeros((tb, pair_w), F32)
        a_cols, w_cols, e_last = [], [], []
        for r in range(2):
            hh = 2 * p + r
            a_col = a[:, hh:hh + 1]
            a_row = a_t[hh:hh + 1, :]
            seg = jnp.where(causal, jnp.exp(jnp.where(causal, a_col - a_row, 0.0)), 0.0)
            m = cb * seg * dt_t[hh:hh + 1, :]
            half = lane_lo if r == 0 else jnp.logical_not(lane_lo)
            y_p = y_p + _nn(m.astype(BF16), jnp.where(half, xp, 0.0).astype(BF16))
            a_last = a[tb - 1:tb, hh:hh + 1]
            a_cols.append(a_col)
            w_cols.append(jnp.exp(a_last - a_col) * dt[:, hh:hh + 1])
            e_last.append(jnp.exp(a_last))
        y_p = y_p + y_inter * jnp.where(lane_lo, jnp.exp(a_cols[0]), jnp.exp(a_cols[1]))
        xw = xp * jnp.where(lane_lo, w_cols[0], w_cols[1])
        st_ref[p] = sp * jnp.where(row_lo, e_last[0], e_last[1]) + _tn(xw.astype(BF16), bg.astype(BF16))
        y_pairs.append(y_p)

    y = jnp.concatenate(y_pairs, axis=-1) + dsk_ref[...] * xs
    yz = y * _silu(z)
    o_ref[...] = (yz * lax.rsqrt(jnp.mean(yz * yz, axis=-1, keepdims=True) + RMS_EPS)
                  * ng_ref[...]).astype(o_ref.dtype)

    @pl.when(t == pl.num_programs(1) - 1)
    def _():
        sout_ref[0] = st_ref[...]


def _ssd(cols_b, nb, t_len, conv_w, conv_b, dt_bias, a_log, d_skip, norm_g, conv_state, ssm_state):
    tb = min(TIME_BLOCK, t_len)
    nt = t_len // tb
    n_pair = B_HEADS // 2
    pair_w = 2 * B_HEADDIM
    pad8 = lambda v: jnp.pad(v.reshape(1, B_HEADS), ((0, 0), (0, B_DT_PAD - B_HEADS)))
    const = lambda a: pl.BlockSpec(a.shape, lambda b, t: (0,) * a.ndim)
    args = (conv_w, conv_b.reshape(1, -1), pad8(dt_bias), pad8(a_log),
            jnp.repeat(d_skip, B_HEADDIM).reshape(1, B_INNER), norm_g.reshape(1, -1))
    state_spec = pl.BlockSpec((1, n_pair, pair_w, B_STATE), lambda b, t: (b, 0, 0, 0))
    o, s_out = pl.pallas_call(
        _ssd_kernel,
        grid=(nb, nt),
        in_specs=[pl.BlockSpec((tb, B_COLS_PAD), lambda b, t: (b * nt + t, 0))]
                 + [const(a) for a in args]
                 + [pl.BlockSpec((1, B_CONV - 1, B_CONV_DIM), lambda b, t: (b, 0, 0)), state_spec],
        out_specs=[pl.BlockSpec((tb, B_INNER), lambda b, t: (b * nt + t, 0)), state_spec],
        out_shape=[jax.ShapeDtypeStruct((nb * t_len, B_INNER), BF16),
                   jax.ShapeDtypeStruct((nb, n_pair, pair_w, B_STATE), F32)],
        scratch_shapes=[pltpu.VMEM((tb + SUBLANES, B_CONV_DIM), F32),
                        pltpu.VMEM((n_pair, pair_w, B_STATE), F32)],
        compiler_params=_params("arbitrary", "arbitrary"),
        name="ssd",
    )(cols_b, *args, conv_state, ssm_state.reshape(nb, n_pair, pair_w, B_STATE))
    return o, s_out.reshape(nb, B_HEADS, B_HEADDIM, B_STATE)


RWKV_PASSES = 3


def _rwkv_kernel(cols_ref, mu_ref, w0_ref, ww2_ref, a0_ref, aw2_ref, gw2_ref, kk_ref, ka_ref, rk_ref,
                 lnw_ref, lnb_ref, sh0_ref, s0_ref, o_ref, sout_ref, ext_ref, st_ref):
    t = pl.program_id(1)
    tb = cols_ref.shape[0]
    n_pair = C_HEADS // 2
    pw = 2 * C_HEADDIM
    s3 = 3 * C_WIDTH
    mm = functools.partial(_mm, passes=RWKV_PASSES)

    @pl.when(t == 0)
    def _():
        ext_ref[0:SUBLANES, :] = jnp.zeros((SUBLANES, C_COLS_PAD), F32)
        ext_ref[SUBLANES - 1:SUBLANES, :] = sh0_ref[0]
        st_ref[...] = s0_ref[0]

    cur = cols_ref[...]
    ext_ref[SUBLANES:SUBLANES + tb, :] = cur
    prev = ext_ref[pl.ds(SUBLANES - 1, tb), :]
    ext_ref[0:SUBLANES, :] = ext_ref[tb:tb + SUBLANES, :]
    mixed = cur + (prev - cur) * mu_ref[...]
    r = mixed[:, 0:C_WIDTH]
    k = mixed[:, C_WIDTH:2 * C_WIDTH]
    v = mixed[:, 2 * C_WIDTH:s3]
    w_lo = mixed[:, s3:s3 + C_LORA_PAD]
    a_lo = mixed[:, s3 + C_LORA_PAD:s3 + 2 * C_LORA_PAD]
    g_lo = mixed[:, s3 + 2 * C_LORA_PAD:]

    w_raw = -_softplus(-(w0_ref[...] + _mm(_nn, jnp.tanh(w_lo), ww2_ref[...], 1))) - 0.5
    logw = -jnp.exp(w_raw)
    a = _sigmoid(a0_ref[...] + _mm(_nn, a_lo, aw2_ref[...], 1))
    g = _mm(_nn, _sigmoid(g_lo), gw2_ref[...], 1)

    head_of = lambda n, d: _iota2((n, n), d) >> (C_HEADDIM.bit_length() - 1)
    seg_ones = jnp.where(head_of(C_WIDTH, 0) == head_of(C_WIDTH, 1), 1.0, 0.0).astype(BF16)
    segsum = lambda x: _exact_rhs_nn(x, seg_ones)

    kk = k * kk_ref[...]
    kk = kk / jnp.maximum(jnp.sqrt(segsum(kk * kk)), 1e-12)
    k2 = k * (1.0 + (a - 1.0) * ka_ref[...])
    beta = kk * a

    c = _exact_lhs_nn(_tril_bf16(tb), logw)
    c_last = c[tb - 1:tb, :]
    gam_last = jnp.exp(c_last)
    g_inv = jnp.exp(-c)
    g_hat = jnp.exp(c_last - c)
    ab_all = -kk * jnp.exp(c - logw)
    rb_all = r * jnp.exp(c)
    bt_all = beta * g_inv
    kt_all = k2 * g_inv
    bh_all = beta * g_hat
    kh_all = k2 * g_hat

    n2 = 2 * tb
    ri = _iota2((n2, n2), 0)
    ci = _iota2((n2, n2), 1)
    top = ri < tb
    left = ci < tb
    strict = (ci & (tb - 1)) < (ri & (tb - 1))
    incl = (ci & (tb - 1)) <= (ri & (tb - 1))
    bot = jnp.logical_not(top)
    right = jnp.logical_not(left)
    m0 = (top & strict) | (bot & incl)
    m1 = (top & incl) | (bot & strict)
    tl = top & left
    br = bot & right
    eye = jnp.where(ri == ci, 1.0, 0.0)
    lane0 = _iota2((1, pw), 1) < C_HEADDIM
    rows_top = _iota2((n2, 1), 0) < tb
    d_own = (rows_top & lane0) | (jnp.logical_not(rows_top) & jnp.logical_not(lane0))
    blockdiag = (_iota2((pw, pw), 0) < C_HEADDIM) == (_iota2((pw, pw), 1) < C_HEADDIM)

    y_pairs = []
    for p in range(n_pair):
        ps = slice(p * pw, (p + 1) * pw)
        ab, rb, bt, kt, bh, kh, vp = (x[:, ps] for x in (ab_all, rb_all, bt_all, kt_all, bh_all, kh_all, v))
        sp = st_ref[p]
        x0 = jnp.concatenate([jnp.where(lane0, ab, 0.0), jnp.where(lane0, rb, 0.0)], axis=0)
        x1 = jnp.concatenate([jnp.where(lane0, 0.0, rb), jnp.where(lane0, 0.0, ab)], axis=0)
        y0 = jnp.concatenate([bt, kt], axis=0)
        y1 = jnp.concatenate([kt, bt], axis=0)
        p0 = jnp.where(m0, mm(_nt, x0, y0), 0.0)
        p1 = jnp.where(m1, mm(_nt, x1, y1), 0.0)
        g0 = mm(_nt, x0, sp)
        g1 = mm(_nt, x1, sp)
        a_bd = jnp.where(tl, p0, 0.0) + jnp.where(br, p1, 0.0)
        q_ak = jnp.where(top & right, p0, 0.0) + jnp.where(bot & left, p1, 0.0)
        vv = jnp.concatenate([vp, vp], axis=0)
        w = jnp.where(rows_top, g0, g1) + jnp.where(d_own, mm(_nn, q_ak, vv), 0.0)
        t_inv = eye + a_bd
        a_pow = a_bd
        steps = max(1, (tb - 1).bit_length()) - 1
        for _ in range(steps):
            a_pow = mm(_nn, a_pow, a_pow)
            t_inv = t_inv + mm(_nn, t_inv, a_pow)
        u = mm(_nn, t_inv, w)
        zz = u + jnp.concatenate([jnp.where(lane0, 0.0, vp), jnp.where(lane0, vp, 0.0)], axis=0)
        r_mat = jnp.where(top, p1, p0)
        yf = jnp.where(d_own, 0.0, mm(_nn, r_mat, zz) + jnp.where(rows_top, g1, g0))
        y_pairs.append(yf[0:tb] + yf[tb:n2])
        up = u[0:tb] + u[tb:n2]
        uv = jnp.concatenate([up, vp], axis=0)
        bk = jnp.concatenate([bh, kh], axis=0)
        st_ref[p] = sp * gam_last[:, ps] + jnp.where(blockdiag, mm(_tn, uv, bk), 0.0)

    y = jnp.concatenate(y_pairs, axis=-1)
    inv_n = 1.0 / C_HEADDIM
    mean = segsum(y) * inv_n
    dlt = y - mean
    var = segsum(dlt * dlt) * inv_n
    yn = dlt * lax.rsqrt(var + C_LN_EPS) * lnw_ref[...] + lnb_ref[...]
    bonus = segsum(r * k2 * rk_ref[...]) * v
    o_ref[...] = ((yn + bonus) * g).astype(o_ref.dtype)

    @pl.when(t == pl.num_programs(1) - 1)
    def _():
        sout_ref[0] = st_ref[...]


def _pad_lora_cols(a):
    s3 = 3 * C_WIDTH
    z = jnp.zeros(a.shape[:-1] + (C_LORA_PAD - C_DECAY_LORA,), a.dtype)
    return jnp.concatenate([a[..., :s3 + C_DECAY_LORA], z,
                            a[..., s3 + C_DECAY_LORA:s3 + C_DECAY_LORA + C_ICLR_LORA], z,
                            a[..., s3 + C_DECAY_LORA + C_ICLR_LORA:]], axis=-1)


def _unpad_lora_cols(a):
    s3 = 3 * C_WIDTH
    return jnp.concatenate([a[..., :s3 + C_DECAY_LORA],
                            a[..., s3 + C_LORA_PAD:s3 + C_LORA_PAD + C_ICLR_LORA],
                            a[..., s3 + 2 * C_LORA_PAD:]], axis=-1)


def _rwkv(cols_c, nb, t_len, shift_state, wkv_state, mu, w0, w_w2, a0, a_w2, g_w2, k_k, k_a, r_k, ln_w, ln_b):
    tb = min(TIME_BLOCK, t_len)
    nt = t_len // tb
    n_pair = C_HEADS // 2
    pw = 2 * C_HEADDIM
    row = lambda v: v.reshape(1, -1)
    pad_rows = lambda w: jnp.pad(w, ((0, C_LORA_PAD - w.shape[0]), (0, 0)))
    args = (row(_pad_lora_cols(mu)), row(w0), pad_rows(w_w2), row(a0), pad_rows(a_w2), g_w2,
            row(k_k), row(k_a), row(r_k), row(ln_w), row(ln_b))
    const = lambda a: pl.BlockSpec(a.shape, lambda b, t: (0,) * a.ndim)
    s4 = wkv_state.reshape(nb, n_pair, 2, C_HEADDIM, C_HEADDIM)
    zero = jnp.zeros_like(s4[:, :, 0])
    s_bd = jnp.concatenate([jnp.concatenate([s4[:, :, 0], zero], axis=-1),
                            jnp.concatenate([zero, s4[:, :, 1]], axis=-1)], axis=-2)
    state_spec = pl.BlockSpec((1, n_pair, pw, pw), lambda b, t: (b, 0, 0, 0))
    o, s_out = pl.pallas_call(
        _rwkv_kernel,
        grid=(nb, nt),
        in_specs=[pl.BlockSpec((tb, C_COLS_PAD), lambda b, t: (b * nt + t, 0))]
                 + [const(a) for a in args]
                 + [pl.BlockSpec((1, 1, C_COLS_PAD), lambda b, t: (b, 0, 0)), state_spec],
        out_specs=[pl.BlockSpec((tb, C_WIDTH), lambda b, t: (b * nt + t, 0)), state_spec],
        out_shape=[jax.ShapeDtypeStruct((nb * t_len, C_WIDTH), BF16),
                   jax.ShapeDtypeStruct((nb, n_pair, pw, pw), F32)],
        scratch_shapes=[pltpu.VMEM((tb + SUBLANES, C_COLS_PAD), F32),
                        pltpu.VMEM((n_pair, pw, pw), F32)],
        compiler_params=_params("arbitrary", "arbitrary"),
        name="rwkv7",
    )(cols_c, *args, _pad_lora_cols(shift_state), s_bd)
    s_new = jnp.stack([s_out[:, :, :C_HEADDIM, :C_HEADDIM], s_out[:, :, C_HEADDIM:, C_HEADDIM:]], axis=2)
    return o, s_new.reshape(nb, C_HEADS, C_HEADDIM, C_HEADDIM)


def _merge_kernel(seqs, x_ref, g_ref, sh_ref, sc_ref, gt_ref, oa_ref, ob_ref, oc_ref,
                  wg_ref, wb_ref, wo_ref, out_ref):
    x = x_ref[...]
    rows = x.shape[0]
    h = _norm_mod(x, g_ref[...], sh_ref[...], sc_ref[...], seqs).astype(BF16)
    merged = jnp.zeros((rows, D_MODEL), F32)
    off = 0
    for i, o_ref in enumerate((oa_ref, ob_ref, oc_ref)):
        width = o_ref.shape[1]
        gate = _sigmoid(_nn(h, wg_ref[:, i * D_MODEL:(i + 1) * D_MODEL]))
        merged = merged + gate * _nn(o_ref[...], wb_ref[off:off + width, :])
        off += width
    mix = _nn(merged.astype(BF16), wo_ref[...])
    out = x.reshape(seqs, rows // seqs, D_MODEL) + gt_ref[...] * mix.reshape(seqs, rows // seqs, D_MODEL)
    out_ref[...] = out.reshape(rows, D_MODEL)


def _merge(x, t_len, g, ada3, o_a, o_b, o_c, wg, wb, wo, tile):
    n = x.shape[0]
    seqs, tps = _tile_plan(t_len, tile)
    full = lambda a: pl.BlockSpec(a.shape, lambda i: (0,) * a.ndim)
    ada_spec = lambda j: pl.BlockSpec((seqs, 1, D_MODEL), lambda i: (i // tps, 0, j))
    row = lambda w: pl.BlockSpec((tile, w), lambda i: (i, 0))
    return pl.pallas_call(
        functools.partial(_merge_kernel, seqs),
        grid=(n // tile,),
        in_specs=[row(D_MODEL), full(g), ada_spec(0), ada_spec(1), ada_spec(2),
                  row(A_WIDTH), row(B_INNER), row(C_WIDTH), full(wg), full(wb), full(wo)],
        out_specs=row(D_MODEL),
        out_shape=jax.ShapeDtypeStruct((n, D_MODEL), F32),
        compiler_params=_params("arbitrary"),
        name="merge",
    )(x, g, ada3, ada3, ada3, o_a, o_b, o_c, wg, wb, wo)


def _route(scores, bias):
    lane = _iota2(scores.shape, 1)
    sel = scores + bias

    def partner(x, d, span):
        wrap = (lane & (span - 1)) + d >= span
        fwd = pltpu.roll(x, x.shape[1] - d, 1)
        back = pltpu.roll(x, span - d, 1)
        return jnp.where(wrap, back, fwd), wrap

    rank = jnp.zeros(scores.shape, jnp.int32)
    for d in range(1, EXPERTS_PER_GROUP):
        other, wrap = partner(sel, d, EXPERTS_PER_GROUP)
        beats = (other > sel) | (wrap & (other == sel))
        rank = rank + jnp.where(beats, 1, 0)
    in_top = rank < 2
    g_score = jnp.where(in_top, sel, 0.0)
    top_s = jnp.where(in_top, scores, 0.0)
    g_sum, w_sum = g_score, top_s
    for d in range(1, EXPERTS_PER_GROUP):
        g_sum = g_sum + partner(g_score, d, EXPERTS_PER_GROUP)[0]
        w_sum = w_sum + partner(top_s, d, EXPERTS_PER_GROUP)[0]
    chosen = in_top & (lane < N_EXPERTS)
    for d in range(EXPERTS_PER_GROUP, N_EXPERTS, EXPERTS_PER_GROUP):
        other, wrap = partner(g_sum, d, N_EXPERTS)
        chosen = chosen & ((other < g_sum) | (jnp.logical_not(wrap) & (other == g_sum)))
    return jnp.where(chosen, scores / w_sum, 0.0)


def _moe_kernel(seqs, final, x_ref, g_ref, sh_ref, sc_ref, gt_ref, wr_ref, br_ref, fg_ref,
                wg_ref, wu_ref, wd_ref, out_ref, hb_ref, coef_ref, acc_ref):
    e = pl.program_id(1)
    rows = x_ref.shape[0]

    @pl.when(e == 0)
    def _():
        h = _norm_mod(x_ref[...], g_ref[...], sh_ref[...], sc_ref[...], seqs)
        hb_ref[...] = h.astype(BF16)
        scores = _sigmoid(_mm(_nn, h, wr_ref[...], 1))
        coef_ref[...] = _route(scores, br_ref[...])
        acc_ref[...] = jnp.zeros_like(acc_ref)

    hb = hb_ref[...]
    coef = coef_ref[...]
    c_e = jnp.sum(jnp.where(_iota2(coef.shape, 1) == e, coef, 0.0), axis=-1, keepdims=True)
    hid = _silu(_nn(hb, wg_ref[...])) * _nn(hb, wu_ref[...])
    y = _nn(hid.astype(BF16), wd_ref[...])
    acc_ref[...] += jnp.where(c_e != 0.0, c_e * y, 0.0)

    @pl.when(e == pl.num_programs(1) - 1)
    def _():
        x = x_ref[...]
        out = x.reshape(seqs, rows // seqs, D_MODEL) + gt_ref[...] * acc_ref[...].reshape(seqs, rows // seqs, D_MODEL)
        out = out.reshape(rows, D_MODEL)
        if final:
            out = out * lax.rsqrt(jnp.mean(out * out, axis=-1, keepdims=True) + RMS_EPS) * fg_ref[...]
        out_ref[...] = out


def _moe(x, t_len, g, ada3, w_router, b_router, final_g, wg, wu, wd, tile, final):
    n = x.shape[0]
    seqs, tps = _tile_plan(t_len, tile)
    full = lambda a: pl.BlockSpec(a.shape, lambda i, e: (0,) * a.ndim)
    ada_spec = lambda j: pl.BlockSpec((seqs, 1, D_MODEL), lambda i, e: (i // tps, 0, j))
    row = pl.BlockSpec((tile, D_MODEL), lambda i, e: (i, 0))
    expert = lambda a: pl.BlockSpec((None,) + a.shape[1:], lambda i, e: (e, 0, 0))
    return pl.pallas_call(
        functools.partial(_moe_kernel, seqs, final),
        grid=(n // tile, N_EXPERTS),
        in_specs=[row, full(g), ada_spec(3), ada_spec(4), ada_spec(5), full(w_router), full(b_router),
                  full(final_g), expert(wg), expert(wu), expert(wd)],
        out_specs=row,
        out_shape=jax.ShapeDtypeStruct((n, D_MODEL), F32),
        scratch_shapes=[pltpu.VMEM((tile, D_MODEL), BF16),
                        pltpu.VMEM((tile, LANES), F32),
                        pltpu.VMEM((tile, D_MODEL), F32)],
        compiler_params=_params("arbitrary", "arbitrary"),
        name="moe",
    )(x, g, ada3, ada3, ada3, w_router, b_router, final_g, wg, wu, wd)


def _prep_weights(w_in):
    o_b = A_COLS
    o_c = A_COLS + B_COLS
    o_g = o_c + C_COLS
    wa = w_in[..., :o_b]
    wb = jnp.pad(w_in[..., o_b:o_c], ((0, 0), (0, 0), (0, B_DT_PAD - B_HEADS)))
    wc = _pad_lora_cols(w_in[..., o_c:o_g])
    wgate = w_in[..., o_g:]
    return tuple(w.astype(BF16) for w in (wa, wb, wc, wgate))


def _trunk(x, ada, states, weights, tiles):
    nb, t_len, _ = x.shape
    st_hgrn, st_ssm, st_conv, st_wkv, st_shift = states
    (wa, wb, wc, wgate, w_branch, w_out, w_gate_e, w_up_e, w_down_e, w_router, b_router, p) = weights
    tile_tok, tile_moe = tiles
    depth = wa.shape[0]
    xf = x.reshape(nb * t_len, D_MODEL)
    row = lambda v: v.reshape(1, -1)
    outs = []
    for l in range(depth):
        ada3 = ada[l].reshape(nb, 1, -1)
        g1 = row(p['norm1_g'][l])
        cols_a, cols_b, cols_c = _inproj(xf, t_len, g1, ada3, wa[l], wb[l], wc[l], tile_tok)
        o_a, hgrn_new = _hgrn(cols_a, nb, t_len, l, p['hgrn_lb'], row(p['hgrn_norm_g'][l]), st_hgrn[l])
        o_b, ssm_new = _ssd(cols_b, nb, t_len, p['ssm_conv_w'][l], p['ssm_conv_b'][l], p['ssm_dt_bias'][l],
                            p['ssm_a_log'][l], p['ssm_d'][l], p['ssm_norm_g'][l], st_conv[l], st_ssm[l])
        o_c, wkv_new = _rwkv(cols_c, nb, t_len, st_shift[l], st_wkv[l], p['rwkv_mu'][l], p['rwkv_w0'][l],
                             p['rwkv_w_w2'][l], p['rwkv_a0'][l], p['rwkv_a_w2'][l], p['rwkv_g_w2'][l],
                             p['rwkv_k_k'][l], p['rwkv_k_a'][l], p['rwkv_r_k'][l].reshape(-1),
                             p['rwkv_ln_w'][l], p['rwkv_ln_b'][l])
        assert t_len >= B_CONV - 1
        conv_new = cols_b.reshape(nb, t_len, B_COLS_PAD)[:, t_len - (B_CONV - 1):, B_INNER:B_INNER + B_CONV_DIM]
        shift_new = _unpad_lora_cols(cols_c.reshape(nb, t_len, C_COLS_PAD)[:, t_len - 1:])
        xf = _merge(xf, t_len, g1, ada3, o_a, o_b, o_c, wgate[l], w_branch[l], w_out[l], tile_tok)
        xf = _moe(xf, t_len, row(p['norm2_g'][l]), ada3, w_router, b_router, row(p['final_g']),
                  w_gate_e[l], w_up_e[l], w_down_e[l], tile_moe, final=(l == depth - 1))
        outs.append((hgrn_new, ssm_new, conv_new, wkv_new, shift_new))
    new_states = [jnp.stack(s, axis=0) for s in zip(*outs)]
    return xf.reshape(nb, t_len, D_MODEL), new_states


def kernel(x_prompt, x_sample, c_prompt, c_sample, state_hgrn, state_ssm, state_conv, state_wkv, state_shift,
           w_ada, b_ada, norm1_g, norm2_g, final_g, w_in, hgrn_lb, hgrn_norm_g,
           ssm_conv_w, ssm_conv_b, ssm_dt_bias, ssm_a_log, ssm_d, ssm_norm_g,
           rwkv_mu, rwkv_w0, rwkv_w_w2, rwkv_a0, rwkv_a_w2, rwkv_g_w2, rwkv_k_k, rwkv_k_a, rwkv_r_k,
           rwkv_ln_w, rwkv_ln_b, w_branch, w_out, w_router, b_router, w_gate_e, w_up_e, w_down_e):
    p = {'norm1_g': norm1_g, 'norm2_g': norm2_g, 'final_g': final_g, 'hgrn_lb': hgrn_lb,
         'hgrn_norm_g': hgrn_norm_g, 'ssm_conv_w': ssm_conv_w, 'ssm_conv_b': ssm_conv_b,
         'ssm_dt_bias': ssm_dt_bias, 'ssm_a_log': ssm_a_log, 'ssm_d': ssm_d, 'ssm_norm_g': ssm_norm_g,
         'rwkv_mu': rwkv_mu, 'rwkv_w0': rwkv_w0, 'rwkv_w_w2': rwkv_w_w2, 'rwkv_a0': rwkv_a0,
         'rwkv_a_w2': rwkv_a_w2, 'rwkv_g_w2': rwkv_g_w2, 'rwkv_k_k': rwkv_k_k, 'rwkv_k_a': rwkv_k_a,
         'rwkv_r_k': rwkv_r_k, 'rwkv_ln_w': rwkv_ln_w, 'rwkv_ln_b': rwkv_ln_b}
    depth = w_in.shape[0]
    bp, tp = x_prompt.shape[0], x_prompt.shape[1]
    bs, ts = x_sample.shape[0], x_sample.shape[1]
    wa, wb, wc, wgate = _prep_weights(w_in)
    w_router_pad = jnp.pad(w_router, ((0, 0), (0, LANES - N_EXPERTS)))
    b_router_pad = jnp.pad(b_router.reshape(1, -1), ((0, 0), (0, LANES - N_EXPERTS)))
    weights = (wa, wb, wc, wgate, w_branch.astype(BF16), w_out.astype(BF16),
               w_gate_e.astype(BF16), w_up_e.astype(BF16), w_down_e.astype(BF16),
               w_router_pad, b_router_pad, p)
    ada = _ada(jnp.concatenate([c_prompt, c_sample], axis=0), w_ada, b_ada)

    zeros_p = (jnp.zeros((depth, bp, A_HEADS, A_DK, A_DV), F32),
               jnp.zeros((depth, bp, B_HEADS, B_HEADDIM, B_STATE), F32),
               jnp.zeros((depth, bp, B_CONV - 1, B_CONV_DIM), x_prompt.dtype),
               jnp.zeros((depth, bp, C_HEADS, C_HEADDIM, C_HEADDIM), F32),
               jnp.zeros((depth, bp, 1, C_COLS), x_prompt.dtype))
    tile_p = min(256, tp)
    y_prompt, st_p = _trunk(x_prompt, ada[:, :bp], zeros_p, weights, (tile_p, min(1024, tp)))
    tile_s = min(256, bs * ts)
    y_sample, st_s = _trunk(x_sample, ada[:, bp:], (state_hgrn, state_ssm, state_conv, state_wkv, state_shift),
                            weights, (tile_s, min(512, bs * ts)))
    return (y_prompt, y_sample, *st_p, *st_s)
```

```python
import functools

import jax
import jax.numpy as jnp
from jax import lax
from jax.experimental import pallas as pl
from jax.experimental.pallas import tpu as pltpu

F32 = jnp.float32
BF16 = jnp.bfloat16

D_MODEL = 1024
RMS_EPS = 1e-6
A_HEADS = 4
A_DK = 128
A_DV = 128
A_WIDTH = A_HEADS * A_DV
A_COLS = 4 * A_WIDTH
LB_FLOOR = 1e-30
B_HEADS = 8
B_HEADDIM = 64
B_INNER = B_HEADS * B_HEADDIM
B_GROUPS = 2
B_HPG = B_HEADS // B_GROUPS
B_STATE = 128
B_CONV = 4
B_CONV_DIM = B_INNER + 2 * B_GROUPS * B_STATE
B_COLS = B_INNER + B_CONV_DIM + B_HEADS
C_HEADS = 8
C_HEADDIM = 64
C_WIDTH = C_HEADS * C_HEADDIM
C_DECAY_LORA = 64
C_ICLR_LORA = 64
C_GATE_LORA = 128
C_COLS = 3 * C_WIDTH + C_DECAY_LORA + C_ICLR_LORA + C_GATE_LORA
C_LN_EPS = 64e-5
N_BRANCH = 3
GATE_COLS = N_BRANCH * D_MODEL
MIX_WIDTH = A_WIDTH + B_INNER + C_WIDTH
N_EXPERTS = 16
N_EXPERT_GROUPS = 4
EXPERTS_PER_GROUP = N_EXPERTS // N_EXPERT_GROUPS
D_FF_EXPERT = 512

LANES = 128
SUBLANES = 8
VMEM_LIMIT = 56 * 1024 * 1024

B_DT_PAD = LANES
B_COLS_PAD = B_INNER + B_CONV_DIM + B_DT_PAD
C_LORA_PAD = LANES
C_COLS_PAD = 3 * C_WIDTH + 3 * C_LORA_PAD

TIME_BLOCK = 64


def _nn(a, b):
    return lax.dot_general(a, b, (((1,), (0,)), ((), ())), preferred_element_type=F32)


def _nt(a, b):
    return lax.dot_general(a, b, (((1,), (1,)), ((), ())), preferred_element_type=F32)


def _tn(a, b):
    return lax.dot_general(a, b, (((0,), (0,)), ((), ())), preferred_element_type=F32)


def _split2(x):
    hi = x.astype(BF16)
    lo = (x - hi.astype(F32)).astype(BF16)
    return hi, lo


def _split3(x):
    hi = x.astype(BF16)
    r1 = x - hi.astype(F32)
    mid = r1.astype(BF16)
    lo = (r1 - mid.astype(F32)).astype(BF16)
    return hi, mid, lo


def _mm(dot, a, b, passes):
    if passes == 1:
        return dot(a.astype(BF16), b.astype(BF16))
    a_hi, a_lo = _split2(a)
    b_hi, b_lo = _split2(b)
    return dot(a_hi, b_hi) + (dot(a_hi, b_lo) + dot(a_lo, b_hi))


def _exact_lhs_nn(m_bf16, x):
    hi, mid, lo = _split3(x)
    return _nn(m_bf16, hi) + (_nn(m_bf16, mid) + _nn(m_bf16, lo))


def _exact_rhs_nn(x, m_bf16):
    hi, mid, lo = _split3(x)
    return _nn(hi, m_bf16) + (_nn(mid, m_bf16) + _nn(lo, m_bf16))


def _sigmoid(x):
    return 1.0 / (1.0 + jnp.exp(-x))


def _silu(x):
    return x * _sigmoid(x)


def _softplus(x):
    return jnp.maximum(x, 0.0) + jnp.log1p(jnp.exp(-jnp.abs(x)))


def _iota2(shape, dim):
    return lax.broadcasted_iota(jnp.int32, shape, dim)


def _tril_bf16(n):
    return jnp.where(_iota2((n, n), 1) <= _iota2((n, n), 0), 1.0, 0.0).astype(BF16)


def _norm_mod(x, g, shift, scale, seqs):
    rows, d = x.shape
    y = x * lax.rsqrt(jnp.mean(x * x, axis=-1, keepdims=True) + RMS_EPS) * g
    y = y.reshape(seqs, rows // seqs, d) * (1.0 + scale) + shift
    return y.reshape(rows, d)


def _tile_plan(t_len, tile):
    if tile >= t_len:
        assert tile % t_len == 0
        return tile // t_len, 1
    assert t_len % tile == 0
    return 1, t_len // tile


def _params(*sem):
    return pltpu.CompilerParams(dimension_semantics=sem, vmem_limit_bytes=VMEM_LIMIT)


def _ada_kernel(c_ref, w_ref, b_ref, o_ref):
    s = _silu(c_ref[...])
    o_ref[...] = _mm(_nn, s, w_ref[...], 1) + b_ref[...]


def _ada(c, w_ada, b_ada):
    nb = c.shape[0]
    depth = w_ada.shape[0]
    n_blk = w_ada.shape[2] // D_MODEL
    return pl.pallas_call(
        _ada_kernel,
        grid=(depth, n_blk),
        in_specs=[pl.BlockSpec((nb, D_MODEL), lambda l, j: (0, 0)),
                  pl.BlockSpec((None, D_MODEL, D_MODEL), lambda l, j: (l, 0, j)),
                  pl.BlockSpec((None, 1, D_MODEL), lambda l, j: (l, 0, j))],
        out_specs=pl.BlockSpec((None, nb, D_MODEL), lambda l, j: (l, 0, j)),
        out_shape=jax.ShapeDtypeStruct((depth, nb, w_ada.shape[2]), F32),
        compiler_params=_params("arbitrary", "arbitrary"),
        name="ada",
    )(c, w_ada, b_ada.reshape(depth, 1, -1))


def _inproj_kernel(seqs, x_ref, g_ref, sh_ref, sc_ref, wa_ref, wb_ref, wc_ref, oa_ref, ob_ref, oc_ref):
    h = _norm_mod(x_ref[...], g_ref[...], sh_ref[...], sc_ref[...], seqs).astype(BF16)
    oa_ref[...] = _nn(h, wa_ref[...])
    ob_ref[...] = _nn(h, wb_ref[...])
    oc_ref[...] = _nn(h, wc_ref[...])


def _inproj(x, t_len, g, ada3, wa, wb, wc, tile):
    n = x.shape[0]
    seqs, tps = _tile_plan(t_len, tile)
    full = lambda a: pl.BlockSpec(a.shape, lambda i: (0,) * a.ndim)
    ada_spec = lambda j: pl.BlockSpec((seqs, 1, D_MODEL), lambda i: (i // tps, 0, j))
    row = lambda w: pl.BlockSpec((tile, w), lambda i: (i, 0))
    return pl.pallas_call(
        functools.partial(_inproj_kernel, seqs),
        grid=(n // tile,),
        in_specs=[row(D_MODEL), full(g), ada_spec(0), ada_spec(1), full(wa), full(wb), full(wc)],
        out_specs=[row(A_COLS), row(B_COLS_PAD), row(C_COLS_PAD)],
        out_shape=[jax.ShapeDtypeStruct((n, A_COLS), F32),
                   jax.ShapeDtypeStruct((n, B_COLS_PAD), F32),
                   jax.ShapeDtypeStruct((n, C_COLS_PAD), F32)],
        compiler_params=_params("arbitrary"),
        name="inproj",
    )(x, g, ada3, ada3, wa, wb, wc)


def _hgrn_kernel(layer, cols_ref, lb_ref, ng_ref, s0_ref, o_ref, sout_ref, st_ref, attn_ref):
    t = pl.program_id(1)
    tb = cols_ref.shape[0]

    @pl.when(t == 0)
    def _():
        for h in range(A_HEADS):
            st_ref[h] = s0_ref[0, h].T

    rows = [lb_ref[i:i + 1, :] for i in range(lb_ref.shape[0])]
    mx = functools.reduce(jnp.maximum, rows)
    es = [jnp.exp(r - mx) for r in rows]
    den = functools.reduce(lambda a, b: a + b, es)
    lower = jnp.zeros_like(mx)
    for i in range(1, layer + 1):
        lower = lower + es[i] / den

    q = cols_ref[:, 0:A_WIDTH]
    f_pre = cols_ref[:, A_WIDTH:2 * A_WIDTH]
    i_val = cols_ref[:, 2 * A_WIDTH:3 * A_WIDTH]
    g = cols_ref[:, 3 * A_WIDTH:4 * A_WIDTH]

    log_lower = jnp.log(jnp.maximum(lower, LB_FLOOR))
    log_sig = jnp.minimum(f_pre, 0.0) - jnp.log1p(jnp.exp(-jnp.abs(f_pre)))
    bterm = jnp.log1p(-lower) + log_sig
    log_f = jnp.maximum(log_lower, bterm) + jnp.log1p(jnp.exp(-jnp.abs(log_lower - bterm)))
    kf = (1.0 - lower) / (1.0 + jnp.exp(f_pre))

    b_all = _exact_lhs_nn(_tril_bf16(tb), log_f)
    heads = range(A_HEADS)
    hsl = [slice(h * A_DK, (h + 1) * A_DK) for h in heads]
    st = [st_ref[h] for h in heads]
    attn_ref[...] = jnp.zeros_like(attn_ref)
    for h in heads:
        qh, kh, bh = q[:, hsl[h]], kf[:, hsl[h]], b_all[:, hsl[h]]
        for tt in range(tb):
            nr = (tt // SUBLANES + 1) * SUBLANES
            live = _iota2((nr, 1), 0) <= tt
            diff = bh[tt:tt + 1, :] - bh[0:nr]
            dec = jnp.where(live, jnp.exp(jnp.where(live, diff, 0.0)), 0.0)
            attn_ref[h, 0:nr, tt:tt + 1] = jnp.sum(qh[tt:tt + 1, :] * kh[0:nr] * dec, axis=-1, keepdims=True)
    for h in heads:
        hs = hsl[h]
        vh = i_val[:, hs].astype(BF16)
        bh = b_all[:, hs]
        oh = (_nt((q[:, hs] * jnp.exp(bh)).astype(BF16), st[h].astype(BF16))
              + _tn(attn_ref[h].astype(BF16), vh))
        b_last = bh[tb - 1:tb, :]
        kdec = kf[:, hs] * jnp.exp(b_last - bh)
        st_ref[h] = st[h] * jnp.exp(b_last) + _tn(vh, kdec.astype(BF16))
        on = oh * lax.rsqrt(jnp.mean(oh * oh, axis=-1, keepdims=True) + RMS_EPS) * ng_ref[:, hs]
        o_ref[:, hs] = (on * _silu(g[:, hs])).astype(o_ref.dtype)

    @pl.when(t == pl.num_programs(1) - 1)
    def _():
        for h in range(A_HEADS):
            sout_ref[0, h] = st_ref[h].T


def _hgrn(cols_a, nb, t_len, layer, hgrn_lb, norm_g, state0):
    tb = min(TIME_BLOCK, t_len)
    nt = t_len // tb
    return pl.pallas_call(
        functools.partial(_hgrn_kernel, layer),
        grid=(nb, nt),
        in_specs=[pl.BlockSpec((tb, A_COLS), lambda b, t: (b * nt + t, 0)),
                  pl.BlockSpec(hgrn_lb.shape, lambda b, t: (0, 0)),
                  pl.BlockSpec((1, A_WIDTH), lambda b, t: (0, 0)),
                  pl.BlockSpec((1, A_HEADS, A_DK, A_DV), lambda b, t: (b, 0, 0, 0))],
        out_specs=[pl.BlockSpec((tb, A_WIDTH), lambda b, t: (b * nt + t, 0)),
                   pl.BlockSpec((1, A_HEADS, A_DK, A_DV), lambda b, t: (b, 0, 0, 0))],
        out_shape=[jax.ShapeDtypeStruct((nb * t_len, A_WIDTH), BF16),
                   jax.ShapeDtypeStruct((nb, A_HEADS, A_DK, A_DV), F32)],
        scratch_shapes=[pltpu.VMEM((A_HEADS, A_DV, A_DK), F32),
                        pltpu.VMEM((A_HEADS, tb, tb), F32)],
        compiler_params=_params("arbitrary", "arbitrary"),
        name="hgrn2",
    )(cols_a, hgrn_lb, norm_g, state0)


def _ssd_kernel(cols_ref, cw_ref, cb_ref, dtb_ref, alog_ref, dsk_ref, ng_ref, cs0_ref, s0_ref,
                o_ref, sout_ref, ext_ref, st_ref):
    t = pl.program_id(1)
    tb = cols_ref.shape[0]
    n_pair = B_HEADS // 2
    pair_w = 2 * B_HEADDIM

    @pl.when(t == 0)
    def _():
        ext_ref[0:SUBLANES, :] = jnp.zeros((SUBLANES, B_CONV_DIM), F32)
        ext_ref[SUBLANES - (B_CONV - 1):SUBLANES, :] = cs0_ref[0]
        st_ref[...] = s0_ref[0]

    z = cols_ref[:, 0:B_INNER]
    ext_ref[SUBLANES:SUBLANES + tb, :] = cols_ref[:, B_INNER:B_INNER + B_CONV_DIM]
    rnd = lambda a: a.astype(BF16).astype(F32)
    conv = None
    for j in range(B_CONV):
        tap = rnd(cw_ref[j:j + 1, :]) * rnd(ext_ref[pl.ds(SUBLANES - (B_CONV - 1) + j, tb), :])
        conv = tap if conv is None else conv + tap
    conv = conv + cb_ref[...]
    ext_ref[0:SUBLANES, :] = ext_ref[tb:tb + SUBLANES, :]
    xbc = _silu(conv)
    xs = xbc[:, 0:B_INNER]
    bm = xbc[:, B_INNER:B_INNER + B_GROUPS * B_STATE]
    cm = xbc[:, B_INNER + B_GROUPS * B_STATE:]

    dt = _softplus(cols_ref[:, B_INNER + B_CONV_DIM:] + dtb_ref[...])
    d_a = dt * (-jnp.exp(alog_ref[...]))
    a = _exact_lhs_nn(_tril_bf16(tb), d_a)
    a_t = a.T
    dt_t = dt.T
    causal = _iota2((tb, tb), 1) <= _iota2((tb, tb), 0)
    lane_lo = _iota2((1, pair_w), 1) < B_HEADDIM
    row_lo = _iota2((pair_w, 1), 0) < B_HEADDIM

    y_pairs = []
    for p in range(n_pair):
        grp = (2 * p) // B_HPG
        gs = slice(grp * B_STATE, (grp + 1) * B_STATE)
        bg = bm[:, gs]
        cg = cm[:, gs]
        cb = _nt(cg.astype(BF16), bg.astype(BF16))
        xp = xs[:, p * pair_w:(p + 1) * pair_w]
        sp = st_ref[p]
        y_inter = _nt(cg.astype(BF16), sp.astype(BF16))
        y_p = jnp.zeros((tb, pair_w), F32)
        a_cols, w_cols, e_last = [], [], []
        for r in range(2):
            hh = 2 * p + r
            a_col = a[:, hh:hh + 1]
            a_row = a_t[hh:hh + 1, :]
            seg = jnp.where(causal, jnp.exp(jnp.where(causal, a_col - a_row, 0.0)), 0.0)
            m = cb * seg * dt_t[hh:hh + 1, :]
            half = lane_lo if r == 0 else jnp.logical_not(lane_lo)
            y_p = y_p + _nn(m.astype(BF16), jnp.where(half, xp, 0.0).astype(BF16))
            a_last = a[tb - 1:tb, hh:hh + 1]
            a_cols.append(a_col)
            w_cols.append(jnp.exp(a_last - a_col) * dt[:, hh:hh + 1])
            e_last.append(jnp.exp(a_last))
        y_p = y_p + y_inter * jnp.where(lane_lo, jnp.exp(a_cols[0]), jnp.exp(a_cols[1]))
        xw = xp * jnp.where(lane_lo, w_cols[0], w_cols[1])
        st_ref[p] = sp * jnp.where(row_lo, e_last[0], e_last[1]) + _tn(xw.astype(BF16), bg.astype(BF16))
        y_pairs.append(y_p)

    y = jnp.concatenate(y_pairs, axis=-1) + dsk_ref[...] * xs
    yz = y * _silu(z)
    o_ref[...] = (yz * lax.rsqrt(jnp.mean(yz * yz, axis=-1, keepdims=True) + RMS_EPS)
                  * ng_ref[...]).astype(o_ref.dtype)

    @pl.when(t == pl.num_programs(1) - 1)
    def _():
        sout_ref[0] = st_ref[...]


def _ssd(cols_b, nb, t_len, conv_w, conv_b, dt_bias, a_log, d_skip, norm_g, conv_state, ssm_state):
    tb = min(TIME_BLOCK, t_len)
    nt = t_len // tb
    n_pair = B_HEADS // 2
    pair_w = 2 * B_HEADDIM
    pad8 = lambda v: jnp.pad(v.reshape(1, B_HEADS), ((0, 0), (0, B_DT_PAD - B_HEADS)))
    const = lambda a: pl.BlockSpec(a.shape, lambda b, t: (0,) * a.ndim)
    args = (conv_w, conv_b.reshape(1, -1), pad8(dt_bias), pad8(a_log),
            jnp.repeat(d_skip, B_HEADDIM).reshape(1, B_INNER), norm_g.reshape(1, -1))
    state_spec = pl.BlockSpec((1, n_pair, pair_w, B_STATE), lambda b, t: (b, 0, 0, 0))
    o, s_out = pl.pallas_call(
        _ssd_kernel,
        grid=(nb, nt),
        in_specs=[pl.BlockSpec((tb, B_COLS_PAD), lambda b, t: (b * nt + t, 0))]
                 + [const(a) for a in args]
                 + [pl.BlockSpec((1, B_CONV - 1, B_CONV_DIM), lambda b, t: (b, 0, 0)), state_spec],
        out_specs=[pl.BlockSpec((tb, B_INNER), lambda b, t: (b * nt + t, 0)), state_spec],
        out_shape=[jax.ShapeDtypeStruct((nb * t_len, B_INNER), BF16),
                   jax.ShapeDtypeStruct((nb, n_pair, pair_w, B_STATE), F32)],
        scratch_shapes=[pltpu.VMEM((tb + SUBLANES, B_CONV_DIM), F32),
                        pltpu.VMEM((n_pair, pair_w, B_STATE), F32)],
        compiler_params=_params("arbitrary", "arbitrary"),
        name="ssd",
    )(cols_b, *args, conv_state, ssm_state.reshape(nb, n_pair, pair_w, B_STATE))
    return o, s_out.reshape(nb, B_HEADS, B_HEADDIM, B_STATE)


RWKV_PASSES = 3


def _rwkv_kernel(cols_ref, mu_ref, w0_ref, ww2_ref, a0_ref, aw2_ref, gw2_ref, kk_ref, ka_ref, rk_ref,
                 lnw_ref, lnb_ref, sh0_ref, s0_ref, o_ref, sout_ref, ext_ref, st_ref):
    t = pl.program_id(1)
    tb = cols_ref.shape[0]
    n_pair = C_HEADS // 2
    pw = 2 * C_HEADDIM
    s3 = 3 * C_WIDTH
    mm = functools.partial(_mm, passes=RWKV_PASSES)

    @pl.when(t == 0)
    def _():
        ext_ref[0:SUBLANES, :] = jnp.zeros((SUBLANES, C_COLS_PAD), F32)
        ext_ref[SUBLANES - 1:SUBLANES, :] = sh0_ref[0]
        st_ref[...] = s0_ref[0]

    cur = cols_ref[...]
    ext_ref[SUBLANES:SUBLANES + tb, :] = cur
    prev = ext_ref[pl.ds(SUBLANES - 1, tb), :]
    ext_ref[0:SUBLANES, :] = ext_ref[tb:tb + SUBLANES, :]
    mixed = cur + (prev - cur) * mu_ref[...]
    r = mixed[:, 0:C_WIDTH]
    k = mixed[:, C_WIDTH:2 * C_WIDTH]
    v = mixed[:, 2 * C_WIDTH:s3]
    w_lo = mixed[:, s3:s3 + C_LORA_PAD]
    a_lo = mixed[:, s3 + C_LORA_PAD:s3 + 2 * C_LORA_PAD]
    g_lo = mixed[:, s3 + 2 * C_LORA_PAD:]

    w_raw = -_softplus(-(w0_ref[...] + _mm(_nn, jnp.tanh(w_lo), ww2_ref[...], 1))) - 0.5
    logw = -jnp.exp(w_raw)
    a = _sigmoid(a0_ref[...] + _mm(_nn, a_lo, aw2_ref[...], 1))
    g = _mm(_nn, _sigmoid(g_lo), gw2_ref[...], 1)

    head_of = lambda n, d: _iota2((n, n), d) >> (C_HEADDIM.bit_length() - 1)
    seg_ones = jnp.where(head_of(C_WIDTH, 0) == head_of(C_WIDTH, 1), 1.0, 0.0).astype(BF16)
    def segsum(x):
        hi, lo = _split2(x)
        return _nn(hi, seg_ones) + _nn(lo, seg_ones)

    kk = k * kk_ref[...]
    kk = kk / jnp.maximum(jnp.sqrt(segsum(kk * kk)), 1e-12)
    k2 = k * (1.0 + (a - 1.0) * ka_ref[...])
    beta = kk * a

    c = _exact_lhs_nn(_tril_bf16(tb), logw)
    c_last = c[tb - 1:tb, :]
    gam_last = jnp.exp(c_last)
    g_inv = jnp.exp(-c)
    g_hat = jnp.exp(c_last - c)
    rnd = lambda a: a.astype(BF16).astype(F32)
    ab_all = -rnd(kk) * jnp.exp(c - logw)
    rb_all = rnd(r) * jnp.exp(c)
    bt_all = beta * g_inv
    kt_all = k2 * g_inv
    bh_all = beta * g_hat
    kh_all = k2 * g_hat

    n2 = 2 * tb
    ri = _iota2((n2, n2), 0)
    ci = _iota2((n2, n2), 1)
    keep = jnp.where(ri < tb, ci & (tb - 1), (ci & (tb - 1)) - 1) < (ri & (tb - 1))
    eye = jnp.where(ri == ci, 1.0, 0.0)
    left_h = _iota2((tb, n2), 1) < tb
    lane0 = _iota2((1, pw), 1) < C_HEADDIM
    rows_top = _iota2((n2, 1), 0) < tb
    d_own = (rows_top & lane0) | (jnp.logical_not(rows_top) & jnp.logical_not(lane0))
    blockdiag = (_iota2((pw, pw), 0) < C_HEADDIM) == (_iota2((pw, pw), 1) < C_HEADDIM)
    stack = lambda a, b: jnp.concatenate([a, b], axis=0)
    pairs = range(n_pair)
    sl = [slice(p * pw, (p + 1) * pw) for p in pairs]

    sp = [st_ref[p] for p in pairs]
    vp = [v[:, s] for s in sl]
    x = [stack(ab_all[:, s], rb_all[:, s]) for s in sl]
    yk = [jnp.concatenate([jnp.where(lane0, bt_all[:, s], 0.0), jnp.where(lane0, kt_all[:, s], 0.0),
                           jnp.where(lane0, 0.0, kt_all[:, s]), jnp.where(lane0, 0.0, bt_all[:, s])], axis=0)
          for s in sl]
    pp = [mm(_nt, x[p], yk[p]) for p in pairs]
    blk0 = [jnp.where(keep, pp[p][:, 0:n2], 0.0) for p in pairs]
    blk1 = [jnp.where(keep, pp[p][:, n2:2 * n2], 0.0) for p in pairs]
    a_bd = [stack(jnp.where(left_h, blk0[p][0:tb], 0.0), jnp.where(left_h, 0.0, blk1[p][0:tb])) for p in pairs]
    q_ak = [stack(jnp.where(left_h, 0.0, blk0[p][0:tb]), jnp.where(left_h, blk1[p][0:tb], 0.0)) for p in pairs]
    r_mat = [stack(blk1[p][tb:n2], blk0[p][tb:n2]) for p in pairs]
    gs = [mm(_nt, x[p], sp[p]) for p in pairs]
    w = [jnp.where(d_own, stack(gs[p][0:tb], gs[p][0:tb]) + mm(_nn, q_ak[p], stack(vp[p], vp[p])), 0.0)
         for p in pairs]
    t_inv = [eye + a_bd[p] for p in pairs]
    a_pow = list(a_bd)
    for _ in range(max(1, (tb - 1).bit_length()) - 1):
        a_pow = [_mm(_nn, a_pow[p], a_pow[p], 1) for p in pairs]
        t_inv = [t_inv[p] + _mm(_nn, t_inv[p], a_pow[p], 1) for p in pairs]
    u = [_mm(_nn, t_inv[p], w[p], 1) for p in pairs]
    res = [w[p] - u[p] + mm(_nn, a_bd[p], u[p]) for p in pairs]
    u = [u[p] + _mm(_nn, t_inv[p], res[p], 1) for p in pairs]
    zz = [u[p] + stack(jnp.where(lane0, 0.0, vp[p]), jnp.where(lane0, vp[p], 0.0)) for p in pairs]
    yf = [jnp.where(d_own, 0.0, mm(_nn, r_mat[p], zz[p]) + stack(gs[p][tb:n2], gs[p][tb:n2])) for p in pairs]
    y_pairs = [yf[p][0:tb] + yf[p][tb:n2] for p in pairs]
    uv = [stack(u[p][0:tb] + u[p][tb:n2], vp[p]) for p in pairs]
    bk = [stack(bh_all[:, s], kh_all[:, s]) for s in sl]
    for p in pairs:
        st_ref[p] = sp[p] * gam_last[:, sl[p]] + jnp.where(blockdiag, mm(_tn, uv[p], bk[p]), 0.0)

    y = jnp.concatenate(y_pairs, axis=-1)
    inv_n = 1.0 / C_HEADDIM
    mean = segsum(y) * inv_n
    dlt = y - mean
    var = segsum(dlt * dlt) * inv_n
    yn = dlt * lax.rsqrt(var + C_LN_EPS) * lnw_ref[...] + lnb_ref[...]
    bonus = segsum(r * k2 * rk_ref[...]) * v
    o_ref[...] = ((yn + bonus) * g).astype(o_ref.dtype)

    @pl.when(t == pl.num_programs(1) - 1)
    def _():
        sout_ref[0] = st_ref[...]


def _pad_lora_cols(a):
    s3 = 3 * C_WIDTH
    z = jnp.zeros(a.shape[:-1] + (C_LORA_PAD - C_DECAY_LORA,), a.dtype)
    return jnp.concatenate([a[..., :s3 + C_DECAY_LORA], z,
                            a[..., s3 + C_DECAY_LORA:s3 + C_DECAY_LORA + C_ICLR_LORA], z,
                            a[..., s3 + C_DECAY_LORA + C_ICLR_LORA:]], axis=-1)


def _unpad_lora_cols(a):
    s3 = 3 * C_WIDTH
    return jnp.concatenate([a[..., :s3 + C_DECAY_LORA],
                            a[..., s3 + C_LORA_PAD:s3 + C_LORA_PAD + C_ICLR_LORA],
                            a[..., s3 + 2 * C_LORA_PAD:]], axis=-1)


def _rwkv(cols_c, nb, t_len, shift_state, wkv_state, mu, w0, w_w2, a0, a_w2, g_w2, k_k, k_a, r_k, ln_w, ln_b):
    tb = min(TIME_BLOCK, t_len)
    nt = t_len // tb
    n_pair = C_HEADS // 2
    pw = 2 * C_HEADDIM
    row = lambda v: v.reshape(1, -1)
    pad_rows = lambda w: jnp.pad(w, ((0, C_LORA_PAD - w.shape[0]), (0, 0)))
    args = (row(_pad_lora_cols(mu)), row(w0), pad_rows(w_w2), row(a0), pad_rows(a_w2), g_w2,
            row(k_k), row(k_a), row(r_k), row(ln_w), row(ln_b))
    const = lambda a: pl.BlockSpec(a.shape, lambda b, t: (0,) * a.ndim)
    s4 = wkv_state.reshape(nb, n_pair, 2, C_HEADDIM, C_HEADDIM)
    zero = jnp.zeros_like(s4[:, :, 0])
    s_bd = jnp.concatenate([jnp.concatenate([s4[:, :, 0], zero], axis=-1),
                            jnp.concatenate([zero, s4[:, :, 1]], axis=-1)], axis=-2)
    state_spec = pl.BlockSpec((1, n_pair, pw, pw), lambda b, t: (b, 0, 0, 0))
    o, s_out = pl.pallas_call(
        _rwkv_kernel,
        grid=(nb, nt),
        in_specs=[pl.BlockSpec((tb, C_COLS_PAD), lambda b, t: (b * nt + t, 0))]
                 + [const(a) for a in args]
                 + [pl.BlockSpec((1, 1, C_COLS_PAD), lambda b, t: (b, 0, 0)), state_spec],
        out_specs=[pl.BlockSpec((tb, C_WIDTH), lambda b, t: (b * nt + t, 0)), state_spec],
        out_shape=[jax.ShapeDtypeStruct((nb * t_len, C_WIDTH), BF16),
                   jax.ShapeDtypeStruct((nb, n_pair, pw, pw), F32)],
        scratch_shapes=[pltpu.VMEM((tb + SUBLANES, C_COLS_PAD), F32),
                        pltpu.VMEM((n_pair, pw, pw), F32)],
        compiler_params=_params("arbitrary", "arbitrary"),
        name="rwkv7",
    )(cols_c, *args, _pad_lora_cols(shift_state), s_bd)
    s_new = jnp.stack([s_out[:, :, :C_HEADDIM, :C_HEADDIM], s_out[:, :, C_HEADDIM:, C_HEADDIM:]], axis=2)
    return o, s_new.reshape(nb, C_HEADS, C_HEADDIM, C_HEADDIM)


def _merge_kernel(seqs, x_ref, g_ref, sh_ref, sc_ref, gt_ref, oa_ref, ob_ref, oc_ref,
                  wg_ref, wb_ref, wo_ref, out_ref):
    x = x_ref[...]
    rows = x.shape[0]
    h = _norm_mod(x, g_ref[...], sh_ref[...], sc_ref[...], seqs).astype(BF16)
    merged = jnp.zeros((rows, D_MODEL), F32)
    off = 0
    for i, o_ref in enumerate((oa_ref, ob_ref, oc_ref)):
        width = o_ref.shape[1]
        gate = _sigmoid(_nn(h, wg_ref[:, i * D_MODEL:(i + 1) * D_MODEL]))
        merged = merged + gate * _nn(o_ref[...], wb_ref[off:off + width, :])
        off += width
    mix = _nn(merged.astype(BF16), wo_ref[...])
    out = x.reshape(seqs, rows // seqs, D_MODEL) + gt_ref[...] * mix.reshape(seqs, rows // seqs, D_MODEL)
    out_ref[...] = out.reshape(rows, D_MODEL)


def _merge(x, t_len, g, ada3, o_a, o_b, o_c, wg, wb, wo, tile):
    n = x.shape[0]
    seqs, tps = _tile_plan(t_len, tile)
    full = lambda a: pl.BlockSpec(a.shape, lambda i: (0,) * a.ndim)
    ada_spec = lambda j: pl.BlockSpec((seqs, 1, D_MODEL), lambda i: (i // tps, 0, j))
    row = lambda w: pl.BlockSpec((tile, w), lambda i: (i, 0))
    return pl.pallas_call(
        functools.partial(_merge_kernel, seqs),
        grid=(n // tile,),
        in_specs=[row(D_MODEL), full(g), ada_spec(0), ada_spec(1), ada_spec(2),
                  row(A_WIDTH), row(B_INNER), row(C_WIDTH), full(wg), full(wb), full(wo)],
        out_specs=row(D_MODEL),
        out_shape=jax.ShapeDtypeStruct((n, D_MODEL), F32),
        compiler_params=_params("arbitrary"),
        name="merge",
    )(x, g, ada3, ada3, ada3, o_a, o_b, o_c, wg, wb, wo)


def _route(scores, bias):
    lane = _iota2(scores.shape, 1)
    sel = scores + bias

    def partner(x, d, span):
        wrap = (lane & (span - 1)) + d >= span
        fwd = pltpu.roll(x, x.shape[1] - d, 1)
        back = pltpu.roll(x, span - d, 1)
        return jnp.where(wrap, back, fwd), wrap

    rank = jnp.zeros(scores.shape, jnp.int32)
    for d in range(1, EXPERTS_PER_GROUP):
        other, wrap = partner(sel, d, EXPERTS_PER_GROUP)
        beats = (other > sel) | (wrap & (other == sel))
        rank = rank + jnp.where(beats, 1, 0)
    in_top = rank < 2
    g_score = jnp.where(in_top, sel, 0.0)
    top_s = jnp.where(in_top, scores, 0.0)
    g_sum, w_sum = g_score, top_s
    for d in range(1, EXPERTS_PER_GROUP):
        g_sum = g_sum + partner(g_score, d, EXPERTS_PER_GROUP)[0]
        w_sum = w_sum + partner(top_s, d, EXPERTS_PER_GROUP)[0]
    chosen = in_top & (lane < N_EXPERTS)
    for d in range(EXPERTS_PER_GROUP, N_EXPERTS, EXPERTS_PER_GROUP):
        other, wrap = partner(g_sum, d, N_EXPERTS)
        chosen = chosen & ((other < g_sum) | (jnp.logical_not(wrap) & (other == g_sum)))
    return jnp.where(chosen, scores / w_sum, 0.0)


def _moe_kernel(seqs, final, x_ref, g_ref, sh_ref, sc_ref, gt_ref, wr_ref, br_ref, fg_ref,
                wg_ref, wu_ref, wd_ref, out_ref, hb_ref, coef_ref, acc_ref):
    e = pl.program_id(1)
    rows = x_ref.shape[0]

    @pl.when(e == 0)
    def _():
        h = _norm_mod(x_ref[...], g_ref[...], sh_ref[...], sc_ref[...], seqs)
        hb_ref[...] = h.astype(BF16)
        scores = _sigmoid(_mm(_nn, h, wr_ref[...], 1))
        coef_ref[...] = _route(scores, br_ref[...])
        acc_ref[...] = jnp.zeros_like(acc_ref)

    hb = hb_ref[...]
    coef = coef_ref[...]
    c_e = jnp.sum(jnp.where(_iota2(coef.shape, 1) == e, coef, 0.0), axis=-1, keepdims=True)
    hid = _silu(_nn(hb, wg_ref[...])) * _nn(hb, wu_ref[...])
    y = _nn(hid.astype(BF16), wd_ref[...])
    acc_ref[...] += jnp.where(c_e != 0.0, c_e * y, 0.0)

    @pl.when(e == pl.num_programs(1) - 1)
    def _():
        x = x_ref[...]
        out = x.reshape(seqs, rows // seqs, D_MODEL) + gt_ref[...] * acc_ref[...].reshape(seqs, rows // seqs, D_MODEL)
        out = out.reshape(rows, D_MODEL)
        if final:
            out = out * lax.rsqrt(jnp.mean(out * out, axis=-1, keepdims=True) + RMS_EPS) * fg_ref[...]
        out_ref[...] = out


def _moe(x, t_len, g, ada3, w_router, b_router, final_g, wg, wu, wd, tile, final):
    n = x.shape[0]
    seqs, tps = _tile_plan(t_len, tile)
    full = lambda a: pl.BlockSpec(a.shape, lambda i, e: (0,) * a.ndim)
    ada_spec = lambda j: pl.BlockSpec((seqs, 1, D_MODEL), lambda i, e: (i // tps, 0, j))
    row = pl.BlockSpec((tile, D_MODEL), lambda i, e: (i, 0))
    expert = lambda a: pl.BlockSpec((None,) + a.shape[1:], lambda i, e: (e, 0, 0))
    return pl.pallas_call(
        functools.partial(_moe_kernel, seqs, final),
        grid=(n // tile, N_EXPERTS),
        in_specs=[row, full(g), ada_spec(3), ada_spec(4), ada_spec(5), full(w_router), full(b_router),
                  full(final_g), expert(wg), expert(wu), expert(wd)],
        out_specs=row,
        out_shape=jax.ShapeDtypeStruct((n, D_MODEL), F32),
        scratch_shapes=[pltpu.VMEM((tile, D_MODEL), BF16),
                        pltpu.VMEM((tile, LANES), F32),
                        pltpu.VMEM((tile, D_MODEL), F32)],
        compiler_params=_params("arbitrary", "arbitrary"),
        name="moe",
    )(x, g, ada3, ada3, ada3, w_router, b_router, final_g, wg, wu, wd)


def _prep_weights(w_in):
    o_b = A_COLS
    o_c = A_COLS + B_COLS
    o_g = o_c + C_COLS
    wa = w_in[..., :o_b]
    wb = jnp.pad(w_in[..., o_b:o_c], ((0, 0), (0, 0), (0, B_DT_PAD - B_HEADS)))
    wc = _pad_lora_cols(w_in[..., o_c:o_g])
    wgate = w_in[..., o_g:]
    return tuple(w.astype(BF16) for w in (wa, wb, wc, wgate))


def _trunk(x, ada, states, weights, tiles):
    nb, t_len, _ = x.shape
    st_hgrn, st_ssm, st_conv, st_wkv, st_shift = states
    (wa, wb, wc, wgate, w_branch, w_out, w_gate_e, w_up_e, w_down_e, w_router, b_router, p) = weights
    tile_tok, tile_moe = tiles
    depth = wa.shape[0]
    xf = x.reshape(nb * t_len, D_MODEL)
    row = lambda v: v.reshape(1, -1)
    outs = []
    for l in range(depth):
        ada3 = ada[l].reshape(nb, 1, -1)
        g1 = row(p['norm1_g'][l])
        cols_a, cols_b, cols_c = _inproj(xf, t_len, g1, ada3, wa[l], wb[l], wc[l], tile_tok)
        o_a, hgrn_new = _hgrn(cols_a, nb, t_len, l, p['hgrn_lb'], row(p['hgrn_norm_g'][l]), st_hgrn[l])
        o_b, ssm_new = _ssd(cols_b, nb, t_len, p['ssm_conv_w'][l], p['ssm_conv_b'][l], p['ssm_dt_bias'][l],
                            p['ssm_a_log'][l], p['ssm_d'][l], p['ssm_norm_g'][l], st_conv[l], st_ssm[l])
        o_c, wkv_new = _rwkv(cols_c, nb, t_len, st_shift[l], st_wkv[l], p['rwkv_mu'][l], p['rwkv_w0'][l],
                             p['rwkv_w_w2'][l], p['rwkv_a0'][l], p['rwkv_a_w2'][l], p['rwkv_g_w2'][l],
                             p['rwkv_k_k'][l], p['rwkv_k_a'][l], p['rwkv_r_k'][l].reshape(-1),
                             p['rwkv_ln_w'][l], p['rwkv_ln_b'][l])
        assert t_len >= B_CONV - 1
        conv_new = cols_b.reshape(nb, t_len, B_COLS_PAD)[:, t_len - (B_CONV - 1):, B_INNER:B_INNER + B_CONV_DIM]
        shift_new = _unpad_lora_cols(cols_c.reshape(nb, t_len, C_COLS_PAD)[:, t_len - 1:])
        xf = _merge(xf, t_len, g1, ada3, o_a, o_b, o_c, wgate[l], w_branch[l], w_out[l], tile_tok)
        xf = _moe(xf, t_len, row(p['norm2_g'][l]), ada3, w_router, b_router, row(p['final_g']),
                  w_gate_e[l], w_up_e[l], w_down_e[l], tile_moe, final=(l == depth - 1))
        outs.append((hgrn_new, ssm_new, conv_new, wkv_new, shift_new))
    new_states = [jnp.stack(s, axis=0) for s in zip(*outs)]
    return xf.reshape(nb, t_len, D_MODEL), new_states


def kernel(x_prompt, x_sample, c_prompt, c_sample, state_hgrn, state_ssm, state_conv, state_wkv, state_shift,
           w_ada, b_ada, norm1_g, norm2_g, final_g, w_in, hgrn_lb, hgrn_norm_g,
           ssm_conv_w, ssm_conv_b, ssm_dt_bias, ssm_a_log, ssm_d, ssm_norm_g,
           rwkv_mu, rwkv_w0, rwkv_w_w2, rwkv_a0, rwkv_a_w2, rwkv_g_w2, rwkv_k_k, rwkv_k_a, rwkv_r_k,
           rwkv_ln_w, rwkv_ln_b, w_branch, w_out, w_router, b_router, w_gate_e, w_up_e, w_down_e):
    p = {'norm1_g': norm1_g, 'norm2_g': norm2_g, 'final_g': final_g, 'hgrn_lb': hgrn_lb,
         'hgrn_norm_g': hgrn_norm_g, 'ssm_conv_w': ssm_conv_w, 'ssm_conv_b': ssm_conv_b,
         'ssm_dt_bias': ssm_dt_bias, 'ssm_a_log': ssm_a_log, 'ssm_d': ssm_d, 'ssm_norm_g': ssm_norm_g,
         'rwkv_mu': rwkv_mu, 'rwkv_w0': rwkv_w0, 'rwkv_w_w2': rwkv_w_w2, 'rwkv_a0': rwkv_a0,
         'rwkv_a_w2': rwkv_a_w2, 'rwkv_g_w2': rwkv_g_w2, 'rwkv_k_k': rwkv_k_k, 'rwkv_k_a': rwkv_k_a,
         'rwkv_r_k': rwkv_r_k, 'rwkv_ln_w': rwkv_ln_w, 'rwkv_ln_b': rwkv_ln_b}
    depth = w_in.shape[0]
    bp, tp = x_prompt.shape[0], x_prompt.shape[1]
    bs, ts = x_sample.shape[0], x_sample.shape[1]
    wa, wb, wc, wgate = _prep_weights(w_in)
    w_router_pad = jnp.pad(w_router, ((0, 0), (0, LANES - N_EXPERTS)))
    b_router_pad = jnp.pad(b_router.reshape(1, -1), ((0, 0), (0, LANES - N_EXPERTS)))
    weights = (wa, wb, wc, wgate, w_branch.astype(BF16), w_out.astype(BF16),
               w_gate_e.astype(BF16), w_up_e.astype(BF16), w_down_e.astype(BF16),
               w_router_pad, b_router_pad, p)
    ada = _ada(jnp.concatenate([c_prompt, c_sample], axis=0), w_ada, b_ada)

    zeros_p = (jnp.zeros((depth, bp, A_HEADS, A_DK, A_DV), F32),
               jnp.zeros((depth, bp, B_HEADS, B_HEADDIM, B_STATE), F32),
               jnp.zeros((depth, bp, B_CONV - 1, B_CONV_DIM), x_prompt.dtype),
               jnp.zeros((depth, bp, C_HEADS, C_HEADDIM, C_HEADDIM), F32),
               jnp.zeros((depth, bp, 1, C_COLS), x_prompt.dtype))
    tile_p = min(256, tp)
    y_prompt, st_p = _trunk(x_prompt, ada[:, :bp], zeros_p, weights, (tile_p, min(1024, tp)))
    tile_s = min(256, bs * ts)
    y_sample, st_s = _trunk(x_sample, ada[:, bp:], (state_hgrn, state_ssm, state_conv, state_wkv, state_shift),
                            weights, (tile_s, min(512, bs * ts)))
    return (y_prompt, y_sample, *st_p, *st_s)
```

```python
import functools

import jax
import jax.numpy as jnp
from jax import lax
from jax.experimental import pallas as pl
from jax.experimental.pallas import tpu as pltpu

F32 = jnp.float32
BF16 = jnp.bfloat16

D_MODEL = 1024
RMS_EPS = 1e-6
A_HEADS = 4
A_DK = 128
A_DV = 128
A_WIDTH = A_HEADS * A_DV
A_COLS = 4 * A_WIDTH
LB_FLOOR = 1e-30
B_HEADS = 8
B_HEADDIM = 64
B_INNER = B_HEADS * B_HEADDIM
B_GROUPS = 2
B_HPG = B_HEADS // B_GROUPS
B_STATE = 128
B_CONV = 4
B_CONV_DIM = B_INNER + 2 * B_GROUPS * B_STATE
B_COLS = B_INNER + B_CONV_DIM + B_HEADS
C_HEADS = 8
C_HEADDIM = 64
C_WIDTH = C_HEADS * C_HEADDIM
C_DECAY_LORA = 64
C_ICLR_LORA = 64
C_GATE_LORA = 128
C_COLS = 3 * C_WIDTH + C_DECAY_LORA + C_ICLR_LORA + C_GATE_LORA
C_LN_EPS = 64e-5
N_BRANCH = 3
GATE_COLS = N_BRANCH * D_MODEL
MIX_WIDTH = A_WIDTH + B_INNER + C_WIDTH
N_EXPERTS = 16
N_EXPERT_GROUPS = 4
EXPERTS_PER_GROUP = N_EXPERTS // N_EXPERT_GROUPS
D_FF_EXPERT = 512

LANES = 128
SUBLANES = 8
VMEM_LIMIT = 56 * 1024 * 1024

B_DT_PAD = LANES
B_COLS_PAD = B_INNER + B_CONV_DIM + B_DT_PAD
C_LORA_PAD = LANES
C_COLS_PAD = 3 * C_WIDTH + 3 * C_LORA_PAD

TIME_BLOCK = 64


def _nn(a, b):
    return lax.dot_general(a, b, (((1,), (0,)), ((), ())), preferred_element_type=F32)


def _nt(a, b):
    return lax.dot_general(a, b, (((1,), (1,)), ((), ())), preferred_element_type=F32)


def _tn(a, b):
    return lax.dot_general(a, b, (((0,), (0,)), ((), ())), preferred_element_type=F32)


def _split2(x):
    hi = x.astype(BF16)
    lo = (x - hi.astype(F32)).astype(BF16)
    return hi, lo


def _split3(x):
    hi = x.astype(BF16)
    r1 = x - hi.astype(F32)
    mid = r1.astype(BF16)
    lo = (r1 - mid.astype(F32)).astype(BF16)
    return hi, mid, lo


def _mm(dot, a, b, passes):
    if passes == 1:
        return dot(a.astype(BF16), b.astype(BF16))
    a_hi, a_lo = _split2(a)
    b_hi, b_lo = _split2(b)
    return dot(a_hi, b_hi) + (dot(a_hi, b_lo) + dot(a_lo, b_hi))


def _exact_lhs_nn(m_bf16, x):
    hi, mid, lo = _split3(x)
    return _nn(m_bf16, hi) + (_nn(m_bf16, mid) + _nn(m_bf16, lo))


def _exact_rhs_nn(x, m_bf16):
    hi, mid, lo = _split3(x)
    return _nn(hi, m_bf16) + (_nn(mid, m_bf16) + _nn(lo, m_bf16))


def _sigmoid(x):
    return 1.0 / (1.0 + jnp.exp(-x))


def _silu(x):
    return x * _sigmoid(x)


def _softplus(x):
    return jnp.maximum(x, 0.0) + jnp.log1p(jnp.exp(-jnp.abs(x)))


def _iota2(shape, dim):
    return lax.broadcasted_iota(jnp.int32, shape, dim)


def _tril_bf16(n):
    return jnp.where(_iota2((n, n), 1) <= _iota2((n, n), 0), 1.0, 0.0).astype(BF16)


def _norm_mod(x, g, shift, scale, seqs):
    rows, d = x.shape
    y = x * lax.rsqrt(jnp.mean(x * x, axis=-1, keepdims=True) + RMS_EPS) * g
    y = y.reshape(seqs, rows // seqs, d) * (1.0 + scale) + shift
    return y.reshape(rows, d)


def _tile_plan(t_len, tile):
    if tile >= t_len:
        assert tile % t_len == 0
        return tile // t_len, 1
    assert t_len % tile == 0
    return 1, t_len // tile


def _seqs_per_step(nb):
    return 2 if nb % 2 == 0 else 1


def _params(*sem):
    return pltpu.CompilerParams(dimension_semantics=sem, vmem_limit_bytes=VMEM_LIMIT)


def _ada_kernel(c_ref, w_ref, b_ref, o_ref):
    s = _silu(c_ref[...])
    o_ref[...] = _mm(_nn, s, w_ref[...], 1) + b_ref[...]


def _ada(c, w_ada, b_ada):
    nb = c.shape[0]
    depth = w_ada.shape[0]
    n_blk = w_ada.shape[2] // D_MODEL
    return pl.pallas_call(
        _ada_kernel,
        grid=(depth, n_blk),
        in_specs=[pl.BlockSpec((nb, D_MODEL), lambda l, j: (0, 0)),
                  pl.BlockSpec((None, D_MODEL, D_MODEL), lambda l, j: (l, 0, j)),
                  pl.BlockSpec((None, 1, D_MODEL), lambda l, j: (l, 0, j))],
        out_specs=pl.BlockSpec((None, nb, D_MODEL), lambda l, j: (l, 0, j)),
        out_shape=jax.ShapeDtypeStruct((depth, nb, w_ada.shape[2]), F32),
        compiler_params=_params("arbitrary", "arbitrary"),
        name="ada",
    )(c, w_ada, b_ada.reshape(depth, 1, -1))


def _inproj_kernel(seqs, x_ref, g_ref, sh_ref, sc_ref, wa_ref, wb_ref, wc_ref, oa_ref, ob_ref, oc_ref):
    h = _norm_mod(x_ref[...], g_ref[...], sh_ref[...], sc_ref[...], seqs).astype(BF16)
    oa_ref[...] = _nn(h, wa_ref[...])
    ob_ref[...] = _nn(h, wb_ref[...])
    oc_ref[...] = _nn(h, wc_ref[...])


def _inproj(x, t_len, g, ada3, wa, wb, wc, tile):
    n = x.shape[0]
    seqs, tps = _tile_plan(t_len, tile)
    full = lambda a: pl.BlockSpec(a.shape, lambda i: (0,) * a.ndim)
    ada_spec = lambda j: pl.BlockSpec((seqs, 1, D_MODEL), lambda i: (i // tps, 0, j))
    row = lambda w: pl.BlockSpec((tile, w), lambda i: (i, 0))
    return pl.pallas_call(
        functools.partial(_inproj_kernel, seqs),
        grid=(n // tile,),
        in_specs=[row(D_MODEL), full(g), ada_spec(0), ada_spec(1), full(wa), full(wb), full(wc)],
        out_specs=[row(A_COLS), row(B_COLS_PAD), row(C_COLS_PAD)],
        out_shape=[jax.ShapeDtypeStruct((n, A_COLS), F32),
                   jax.ShapeDtypeStruct((n, B_COLS_PAD), F32),
                   jax.ShapeDtypeStruct((n, C_COLS_PAD), F32)],
        compiler_params=_params("arbitrary"),
        name="inproj",
    )(x, g, ada3, ada3, wa, wb, wc)


def _hgrn_kernel(layer, cols_ref, lb_ref, ng_ref, s0_ref, o_ref, sout_ref, st_ref, attn_ref):
    t = pl.program_id(1)
    nseq, tb = cols_ref.shape[0], cols_ref.shape[1]
    unit = [(s, h) for s in range(nseq) for h in range(A_HEADS)]

    @pl.when(t == 0)
    def _():
        for s, h in unit:
            st_ref[s, h] = s0_ref[s, h].T

    rows = [lb_ref[i:i + 1, :] for i in range(lb_ref.shape[0])]
    mx = functools.reduce(jnp.maximum, rows)
    es = [jnp.exp(r - mx) for r in rows]
    den = functools.reduce(lambda a, b: a + b, es)
    lower = jnp.zeros_like(mx)
    for i in range(1, layer + 1):
        lower = lower + es[i] / den

    log_lower = jnp.log(jnp.maximum(lower, LB_FLOOR))
    log1m_lower = jnp.log1p(-lower)
    tril = _tril_bf16(tb)

    def prep(s):
        f_pre = cols_ref[s, :, A_WIDTH:2 * A_WIDTH]
        log_sig = jnp.minimum(f_pre, 0.0) - jnp.log1p(jnp.exp(-jnp.abs(f_pre)))
        bterm = log1m_lower + log_sig
        log_f = jnp.maximum(log_lower, bterm) + jnp.log1p(jnp.exp(-jnp.abs(log_lower - bterm)))
        return dict(q=cols_ref[s, :, 0:A_WIDTH], k=(1.0 - lower) / (1.0 + jnp.exp(f_pre)),
                    v=cols_ref[s, :, 2 * A_WIDTH:3 * A_WIDTH].astype(BF16),
                    b=_exact_lhs_nn(tril, log_f))

    seq = [prep(s) for s in range(nseq)]
    units = range(len(unit))
    col = lambda name: [seq[s][name][:, h * A_DK:(h + 1) * A_DK] for s, h in unit]
    q, kf, vh, bh = col('q'), col('k'), col('v'), col('b')
    st = [st_ref[s, h] for s, h in unit]

    blk = SUBLANES
    nblk = tb // blk
    rowblk = _iota2((tb, 1), 0) >> (blk.bit_length() - 1)
    attn_ref[...] = jnp.zeros_like(attn_ref)
    att_off = []
    for u in units:
        k2 = jnp.concatenate([kf[u][j * blk:(j + 1) * blk]
                              * jnp.exp(bh[u][(j + 1) * blk - 1:(j + 1) * blk] - bh[u][j * blk:(j + 1) * blk])
                              for j in range(nblk)], axis=0)
        acc = None
        for j in range(nblk - 1):
            r0 = (j + 1) * blk
            qe = q[u][r0:] * jnp.exp(bh[u][r0:] - bh[u][r0 - 1:r0])
            part = _mm(_nt, qe, jnp.where(rowblk == j, k2, 0.0), 3)
            part = jnp.concatenate([jnp.zeros((r0, tb), F32), part], axis=0)
            acc = part if acc is None else acc + part
        att_off.append(acc)
        s_, h_ = unit[u]
        for sc in range(tb):
            j0 = (sc // blk) * blk
            live = _iota2((blk, 1), 0) >= sc - j0
            diff = bh[u][j0:j0 + blk] - bh[u][sc:sc + 1]
            dec = jnp.where(live, jnp.exp(jnp.where(live, diff, 0.0)), 0.0)
            attn_ref[s_, h_, j0:j0 + blk, sc:sc + 1] = jnp.sum(
                q[u][j0:j0 + blk] * kf[u][sc:sc + 1] * dec, axis=-1, keepdims=True)

    g_all = [cols_ref[s, :, 3 * A_WIDTH:4 * A_WIDTH] for s in range(nseq)]
    for u in units:
        s_, h_ = unit[u]
        hs = slice(h_ * A_DV, (h_ + 1) * A_DV)
        att = attn_ref[s_, h_] + att_off[u]
        oh = (_nt((q[u] * jnp.exp(bh[u])).astype(BF16), st[u].astype(BF16))
              + _nn(att.astype(BF16), vh[u]))
        b_last = bh[u][tb - 1:tb]
        kdec = kf[u] * jnp.exp(b_last - bh[u])
        st_ref[s_, h_] = st[u] * jnp.exp(b_last) + _tn(vh[u], kdec.astype(BF16))
        on = oh * lax.rsqrt(jnp.mean(oh * oh, axis=-1, keepdims=True) + RMS_EPS) * ng_ref[:, hs]
        o_ref[s_, :, hs] = (on * _silu(g_all[s_][:, hs])).astype(o_ref.dtype)

    @pl.when(t == pl.num_programs(1) - 1)
    def _():
        for s, h in unit:
            sout_ref[s, h] = st_ref[s, h].T


def _hgrn(cols_a, nb, t_len, layer, hgrn_lb, norm_g, state0):
    tb = min(TIME_BLOCK, t_len)
    nt = t_len // tb
    nseq = _seqs_per_step(nb)
    state_spec = pl.BlockSpec((nseq, A_HEADS, A_DK, A_DV), lambda b, t: (b, 0, 0, 0))
    o, s_out = pl.pallas_call(
        functools.partial(_hgrn_kernel, layer),
        grid=(nb // nseq, nt),
        in_specs=[pl.BlockSpec((nseq, tb, A_COLS), lambda b, t: (b, t, 0)),
                  pl.BlockSpec(hgrn_lb.shape, lambda b, t: (0, 0)),
                  pl.BlockSpec((1, A_WIDTH), lambda b, t: (0, 0)),
                  state_spec],
        out_specs=[pl.BlockSpec((nseq, tb, A_WIDTH), lambda b, t: (b, t, 0)), state_spec],
        out_shape=[jax.ShapeDtypeStruct((nb, t_len, A_WIDTH), BF16),
                   jax.ShapeDtypeStruct((nb, A_HEADS, A_DK, A_DV), F32)],
        scratch_shapes=[pltpu.VMEM((nseq, A_HEADS, A_DV, A_DK), F32),
                        pltpu.VMEM((nseq, A_HEADS, tb, tb), F32)],
        compiler_params=_params("arbitrary", "arbitrary"),
        name="hgrn2",
    )(cols_a.reshape(nb, t_len, A_COLS), hgrn_lb, norm_g, state0)
    return o.reshape(nb * t_len, A_WIDTH), s_out


def _ssd_kernel(cols_ref, cw_ref, cb_ref, dtb_ref, alog_ref, dsk_ref, ng_ref, cs0_ref, s0_ref,
                o_ref, sout_ref, ext_ref, st_ref):
    t = pl.program_id(1)
    nseq, tb = cols_ref.shape[0], cols_ref.shape[1]
    n_pair = B_HEADS // 2
    pair_w = 2 * B_HEADDIM

    @pl.when(t == 0)
    def _():
        ext_ref[:, 0:SUBLANES, :] = jnp.zeros((nseq, SUBLANES, B_CONV_DIM), F32)
        ext_ref[:, SUBLANES - (B_CONV - 1):SUBLANES, :] = cs0_ref[...]
        st_ref[...] = s0_ref[...]

    rnd = lambda a: a.astype(BF16).astype(F32)
    tril = _tril_bf16(tb)
    causal = _iota2((tb, tb), 1) <= _iota2((tb, tb), 0)
    lane_lo = _iota2((1, pair_w), 1) < B_HEADDIM
    row_lo = _iota2((pair_w, 1), 0) < B_HEADDIM
    neg_a = -jnp.exp(alog_ref[...])
    states = [[st_ref[s, p] for p in range(n_pair)] for s in range(nseq)]
    new_states = []

    for s in range(nseq):
        z = cols_ref[s, :, 0:B_INNER]
        ext_ref[s, SUBLANES:SUBLANES + tb, :] = cols_ref[s, :, B_INNER:B_INNER + B_CONV_DIM]
        conv = None
        for j in range(B_CONV):
            tap = rnd(cw_ref[j:j + 1, :]) * rnd(ext_ref[s, pl.ds(SUBLANES - (B_CONV - 1) + j, tb), :])
            conv = tap if conv is None else conv + tap
        conv = conv + cb_ref[...]
        ext_ref[s, 0:SUBLANES, :] = ext_ref[s, tb:tb + SUBLANES, :]
        xbc = _silu(conv)
        xs = xbc[:, 0:B_INNER]
        bm = xbc[:, B_INNER:B_INNER + B_GROUPS * B_STATE].astype(BF16)
        cm = xbc[:, B_INNER + B_GROUPS * B_STATE:].astype(BF16)

        dt = _softplus(cols_ref[s, :, B_INNER + B_CONV_DIM:] + dtb_ref[...])
        a = _exact_lhs_nn(tril, dt * neg_a)
        a_t = a.T
        dt_t = dt.T
        cb = [_nt(cm[:, g * B_STATE:(g + 1) * B_STATE], bm[:, g * B_STATE:(g + 1) * B_STATE])
              for g in range(B_GROUPS)]

        y_pairs = []
        for p in range(n_pair):
            grp = (2 * p) // B_HPG
            bg = bm[:, grp * B_STATE:(grp + 1) * B_STATE]
            cg = cm[:, grp * B_STATE:(grp + 1) * B_STATE]
            xp = xs[:, p * pair_w:(p + 1) * pair_w]
            sp = states[s][p]
            y_p = _nt(cg, sp.astype(BF16))
            a_cols, w_cols, e_last = [], [], []
            for r in range(2):
                hh = 2 * p + r
                a_col = a[:, hh:hh + 1]
                a_last = a[tb - 1:tb, hh:hh + 1]
                a_cols.append(a_col)
                w_cols.append(jnp.exp(a_last - a_col) * dt[:, hh:hh + 1])
                e_last.append(jnp.exp(a_last))
            y_p = y_p * jnp.where(lane_lo, jnp.exp(a_cols[0]), jnp.exp(a_cols[1]))
            for r in range(2):
                hh = 2 * p + r
                seg = jnp.where(causal, jnp.exp(jnp.where(causal, a_cols[r] - a_t[hh:hh + 1, :], 0.0)), 0.0)
                m = cb[grp] * seg * dt_t[hh:hh + 1, :]
                half = lane_lo if r == 0 else jnp.logical_not(lane_lo)
                y_p = y_p + _nn(m.astype(BF16), jnp.where(half, xp, 0.0).astype(BF16))
            xw = xp * jnp.where(lane_lo, w_cols[0], w_cols[1])
            new_states.append(sp * jnp.where(row_lo, e_last[0], e_last[1]) + _tn(xw.astype(BF16), bg))
            y_pairs.append(y_p)

        y = jnp.concatenate(y_pairs, axis=-1) + dsk_ref[...] * xs
        yz = y * _silu(z)
        o_ref[s] = (yz * lax.rsqrt(jnp.mean(yz * yz, axis=-1, keepdims=True) + RMS_EPS)
                    * ng_ref[...]).astype(o_ref.dtype)

    for s in range(nseq):
        for p in range(n_pair):
            st_ref[s, p] = new_states[s * n_pair + p]

    @pl.when(t == pl.num_programs(1) - 1)
    def _():
        sout_ref[...] = st_ref[...]


def _ssd(cols_b, nb, t_len, conv_w, conv_b, dt_bias, a_log, d_skip, norm_g, conv_state, ssm_state):
    tb = min(TIME_BLOCK, t_len)
    nt = t_len // tb
    n_pair = B_HEADS // 2
    pair_w = 2 * B_HEADDIM
    pad8 = lambda v: jnp.pad(v.reshape(1, B_HEADS), ((0, 0), (0, B_DT_PAD - B_HEADS)))
    const = lambda a: pl.BlockSpec(a.shape, lambda b, t: (0,) * a.ndim)
    args = (conv_w, conv_b.reshape(1, -1), pad8(dt_bias), pad8(a_log),
            jnp.repeat(d_skip, B_HEADDIM).reshape(1, B_INNER), norm_g.reshape(1, -1))
    nseq = _seqs_per_step(nb)
    state_spec = pl.BlockSpec((nseq, n_pair, pair_w, B_STATE), lambda b, t: (b, 0, 0, 0))
    o, s_out = pl.pallas_call(
        _ssd_kernel,
        grid=(nb // nseq, nt),
        in_specs=[pl.BlockSpec((nseq, tb, B_COLS_PAD), lambda b, t: (b, t, 0))]
                 + [const(a) for a in args]
                 + [pl.BlockSpec((nseq, B_CONV - 1, B_CONV_DIM), lambda b, t: (b, 0, 0)), state_spec],
        out_specs=[pl.BlockSpec((nseq, tb, B_INNER), lambda b, t: (b, t, 0)), state_spec],
        out_shape=[jax.ShapeDtypeStruct((nb, t_len, B_INNER), BF16),
                   jax.ShapeDtypeStruct((nb, n_pair, pair_w, B_STATE), F32)],
        scratch_shapes=[pltpu.VMEM((nseq, tb + SUBLANES, B_CONV_DIM), F32),
                        pltpu.VMEM((nseq, n_pair, pair_w, B_STATE), F32)],
        compiler_params=_params("arbitrary", "arbitrary"),
        name="ssd",
    )(cols_b.reshape(nb, t_len, B_COLS_PAD), *args, conv_state, ssm_state.reshape(nb, n_pair, pair_w, B_STATE))
    return o.reshape(nb * t_len, B_INNER), s_out.reshape(nb, B_HEADS, B_HEADDIM, B_STATE)


RWKV_PASSES = 3


def _rwkv_kernel(cols_ref, mu_ref, w0_ref, ww2_ref, a0_ref, aw2_ref, gw2_ref, kk_ref, ka_ref, rk_ref,
                 lnw_ref, lnb_ref, sh0_ref, s0_ref, o_ref, sout_ref, ext_ref, st_ref):
    t = pl.program_id(1)
    nseq, tb = cols_ref.shape[0], cols_ref.shape[1]
    n_pair = C_HEADS // 2
    pw = 2 * C_HEADDIM
    s3 = 3 * C_WIDTH
    mm = functools.partial(_mm, passes=RWKV_PASSES)

    @pl.when(t == 0)
    def _():
        ext_ref[:, 0:SUBLANES, :] = jnp.zeros((nseq, SUBLANES, C_COLS_PAD), F32)
        ext_ref[:, SUBLANES - 1:SUBLANES, :] = sh0_ref[...]
        st_ref[...] = s0_ref[...]

    head_of = lambda n, d: _iota2((n, n), d) >> (C_HEADDIM.bit_length() - 1)
    seg_ones = jnp.where(head_of(C_WIDTH, 0) == head_of(C_WIDTH, 1), 1.0, 0.0).astype(BF16)

    def segsum(x):
        hi, lo = _split2(x)
        return _nn(hi, seg_ones) + _nn(lo, seg_ones)

    tril = _tril_bf16(tb)
    rnd = lambda a: a.astype(BF16).astype(F32)

    def prep(s):
        cur = cols_ref[s]
        ext_ref[s, SUBLANES:SUBLANES + tb, :] = cur
        prev = ext_ref[s, pl.ds(SUBLANES - 1, tb), :]
        ext_ref[s, 0:SUBLANES, :] = ext_ref[s, tb:tb + SUBLANES, :]
        mixed = cur + (prev - cur) * mu_ref[...]
        r = mixed[:, 0:C_WIDTH]
        k = mixed[:, C_WIDTH:2 * C_WIDTH]
        v = mixed[:, 2 * C_WIDTH:s3]
        w_lo = mixed[:, s3:s3 + C_LORA_PAD]
        a_lo = mixed[:, s3 + C_LORA_PAD:s3 + 2 * C_LORA_PAD]
        g_lo = mixed[:, s3 + 2 * C_LORA_PAD:]
        w_raw = -_softplus(-(w0_ref[...] + _mm(_nn, jnp.tanh(w_lo), ww2_ref[...], 1))) - 0.5
        logw = -jnp.exp(w_raw)
        a = _sigmoid(a0_ref[...] + _mm(_nn, a_lo, aw2_ref[...], 1))
        g = _mm(_nn, _sigmoid(g_lo), gw2_ref[...], 1)
        kk = k * kk_ref[...]
        kk = kk / jnp.maximum(jnp.sqrt(segsum(kk * kk)), 1e-12)
        k2 = k * (1.0 + (a - 1.0) * ka_ref[...])
        beta = kk * a
        c = _exact_lhs_nn(tril, logw)
        c_last = c[tb - 1:tb, :]
        g_inv = jnp.exp(-c)
        g_hat = jnp.exp(c_last - c)
        return dict(r=r, v=v, g=g, k2=k2, gam_last=jnp.exp(c_last),
                    ab=-rnd(kk) * jnp.exp(c - logw), rb=rnd(r) * jnp.exp(c),
                    bt=beta * g_inv, kt=k2 * g_inv, bh=beta * g_hat, kh=k2 * g_hat)

    seq = [prep(s) for s in range(nseq)]

    n2 = 2 * tb
    ri = _iota2((n2, n2), 0)
    ci = _iota2((n2, n2), 1)
    keep = jnp.where(ri < tb, ci & (tb - 1), (ci & (tb - 1)) - 1) < (ri & (tb - 1))
    eye = jnp.where(ri == ci, 1.0, 0.0)
    left_h = _iota2((tb, n2), 1) < tb
    lane0 = _iota2((1, pw), 1) < C_HEADDIM
    rows_top = _iota2((n2, 1), 0) < tb
    d_own = (rows_top & lane0) | (jnp.logical_not(rows_top) & jnp.logical_not(lane0))
    blockdiag = (_iota2((pw, pw), 0) < C_HEADDIM) == (_iota2((pw, pw), 1) < C_HEADDIM)
    stack = lambda a, b: jnp.concatenate([a, b], axis=0)
    unit = [(s, p) for s in range(nseq) for p in range(n_pair)]
    pairs = range(len(unit))
    col = lambda name: [seq[s][name][:, p * pw:(p + 1) * pw] for s, p in unit]
    sp = [st_ref[s, p] for s, p in unit]
    vp, ab, rb, bt, kt = col('v'), col('ab'), col('rb'), col('bt'), col('kt')
    x = [stack(ab[p], rb[p]) for p in pairs]
    yk = [jnp.concatenate([jnp.where(lane0, bt[p], 0.0), jnp.where(lane0, kt[p], 0.0),
                           jnp.where(lane0, 0.0, kt[p]), jnp.where(lane0, 0.0, bt[p])], axis=0)
          for p in pairs]
    pp = [mm(_nt, x[p], yk[p]) for p in pairs]
    blk0 = [jnp.where(keep, pp[p][:, 0:n2], 0.0) for p in pairs]
    blk1 = [jnp.where(keep, pp[p][:, n2:2 * n2], 0.0) for p in pairs]
    a_bd = [stack(jnp.where(left_h, blk0[p][0:tb], 0.0), jnp.where(left_h, 0.0, blk1[p][0:tb])) for p in pairs]
    q_ak = [stack(jnp.where(left_h, 0.0, blk0[p][0:tb]), jnp.where(left_h, blk1[p][0:tb], 0.0)) for p in pairs]
    r_mat = [stack(blk1[p][tb:n2], blk0[p][tb:n2]) for p in pairs]
    gs = [mm(_nt, x[p], sp[p]) for p in pairs]
    w = [jnp.where(d_own, stack(gs[p][0:tb], gs[p][0:tb]) + mm(_nn, q_ak[p], stack(vp[p], vp[p])), 0.0)
         for p in pairs]
    t_inv = [eye + a_bd[p] for p in pairs]
    a_pow = list(a_bd)
    for _ in range(max(1, (tb - 1).bit_length()) - 1):
        a_pow = [_mm(_nn, a_pow[p], a_pow[p], 1) for p in pairs]
        t_inv = [t_inv[p] + _mm(_nn, t_inv[p], a_pow[p], 1) for p in pairs]
    u = [_mm(_nn, t_inv[p], w[p], 1) for p in pairs]
    res = [w[p] - u[p] + mm(_nn, a_bd[p], u[p]) for p in pairs]
    u = [u[p] + _mm(_nn, t_inv[p], res[p], 1) for p in pairs]
    zz = [u[p] + stack(jnp.where(lane0, 0.0, vp[p]), jnp.where(lane0, vp[p], 0.0)) for p in pairs]
    yf = [jnp.where(d_own, 0.0, mm(_nn, r_mat[p], zz[p]) + stack(gs[p][tb:n2], gs[p][tb:n2])) for p in pairs]
    y_pairs = [yf[p][0:tb] + yf[p][tb:n2] for p in pairs]
    uv = [stack(u[p][0:tb] + u[p][tb:n2], vp[p]) for p in pairs]
    bh, kh, gam = col('bh'), col('kh'), col('gam_last')
    s_new = [sp[p] * gam[p] + jnp.where(blockdiag, mm(_tn, uv[p], stack(bh[p], kh[p])), 0.0) for p in pairs]
    for i, (s, p) in enumerate(unit):
        st_ref[s, p] = s_new[i]

    inv_n = 1.0 / C_HEADDIM
    for s in range(nseq):
        d = seq[s]
        y = jnp.concatenate(y_pairs[s * n_pair:(s + 1) * n_pair], axis=-1)
        mean = segsum(y) * inv_n
        dlt = y - mean
        var = segsum(dlt * dlt) * inv_n
        yn = dlt * lax.rsqrt(var + C_LN_EPS) * lnw_ref[...] + lnb_ref[...]
        bonus = segsum(d['r'] * d['k2'] * rk_ref[...]) * d['v']
        o_ref[s] = ((yn + bonus) * d['g']).astype(o_ref.dtype)

    @pl.when(t == pl.num_programs(1) - 1)
    def _():
        sout_ref[...] = st_ref[...]


def _pad_lora_cols(a):
    s3 = 3 * C_WIDTH
    z = jnp.zeros(a.shape[:-1] + (C_LORA_PAD - C_DECAY_LORA,), a.dtype)
    return jnp.concatenate([a[..., :s3 + C_DECAY_LORA], z,
                            a[..., s3 + C_DECAY_LORA:s3 + C_DECAY_LORA + C_ICLR_LORA], z,
                            a[..., s3 + C_DECAY_LORA + C_ICLR_LORA:]], axis=-1)


def _unpad_lora_cols(a):
    s3 = 3 * C_WIDTH
    return jnp.concatenate([a[..., :s3 + C_DECAY_LORA],
                            a[..., s3 + C_LORA_PAD:s3 + C_LORA_PAD + C_ICLR_LORA],
                            a[..., s3 + 2 * C_LORA_PAD:]], axis=-1)


def _rwkv(cols_c, nb, t_len, shift_state, wkv_state, mu, w0, w_w2, a0, a_w2, g_w2, k_k, k_a, r_k, ln_w, ln_b):
    tb = min(TIME_BLOCK, t_len)
    nt = t_len // tb
    n_pair = C_HEADS // 2
    pw = 2 * C_HEADDIM
    row = lambda v: v.reshape(1, -1)
    pad_rows = lambda w: jnp.pad(w, ((0, C_LORA_PAD - w.shape[0]), (0, 0)))
    args = (row(_pad_lora_cols(mu)), row(w0), pad_rows(w_w2), row(a0), pad_rows(a_w2), g_w2,
            row(k_k), row(k_a), row(r_k), row(ln_w), row(ln_b))
    const = lambda a: pl.BlockSpec(a.shape, lambda b, t: (0,) * a.ndim)
    s4 = wkv_state.reshape(nb, n_pair, 2, C_HEADDIM, C_HEADDIM)
    zero = jnp.zeros_like(s4[:, :, 0])
    s_bd = jnp.concatenate([jnp.concatenate([s4[:, :, 0], zero], axis=-1),
                            jnp.concatenate([zero, s4[:, :, 1]], axis=-1)], axis=-2)
    nseq = _seqs_per_step(nb)
    state_spec = pl.BlockSpec((nseq, n_pair, pw, pw), lambda b, t: (b, 0, 0, 0))
    o, s_out = pl.pallas_call(
        _rwkv_kernel,
        grid=(nb // nseq, nt),
        in_specs=[pl.BlockSpec((nseq, tb, C_COLS_PAD), lambda b, t: (b, t, 0))]
                 + [const(a) for a in args]
                 + [pl.BlockSpec((nseq, 1, C_COLS_PAD), lambda b, t: (b, 0, 0)), state_spec],
        out_specs=[pl.BlockSpec((nseq, tb, C_WIDTH), lambda b, t: (b, t, 0)), state_spec],
        out_shape=[jax.ShapeDtypeStruct((nb, t_len, C_WIDTH), BF16),
                   jax.ShapeDtypeStruct((nb, n_pair, pw, pw), F32)],
        scratch_shapes=[pltpu.VMEM((nseq, tb + SUBLANES, C_COLS_PAD), F32),
                        pltpu.VMEM((nseq, n_pair, pw, pw), F32)],
        compiler_params=_params("arbitrary", "arbitrary"),
        name="rwkv7",
    )(cols_c.reshape(nb, t_len, C_COLS_PAD), *args, _pad_lora_cols(shift_state), s_bd)
    s_new = jnp.stack([s_out[:, :, :C_HEADDIM, :C_HEADDIM], s_out[:, :, C_HEADDIM:, C_HEADDIM:]], axis=2)
    return o.reshape(nb * t_len, C_WIDTH), s_new.reshape(nb, C_HEADS, C_HEADDIM, C_HEADDIM)


def _merge_kernel(seqs, x_ref, g_ref, sh_ref, sc_ref, gt_ref, oa_ref, ob_ref, oc_ref,
                  wg_ref, wb_ref, wo_ref, out_ref):
    x = x_ref[...]
    rows = x.shape[0]
    h = _norm_mod(x, g_ref[...], sh_ref[...], sc_ref[...], seqs).astype(BF16)
    merged = jnp.zeros((rows, D_MODEL), F32)
    off = 0
    for i, o_ref in enumerate((oa_ref, ob_ref, oc_ref)):
        width = o_ref.shape[1]
        gate = _sigmoid(_nn(h, wg_ref[:, i * D_MODEL:(i + 1) * D_MODEL]))
        merged = merged + gate * _nn(o_ref[...], wb_ref[off:off + width, :])
        off += width
    mix = _nn(merged.astype(BF16), wo_ref[...])
    out = x.reshape(seqs, rows // seqs, D_MODEL) + gt_ref[...] * mix.reshape(seqs, rows // seqs, D_MODEL)
    out_ref[...] = out.reshape(rows, D_MODEL)


def _merge(x, t_len, g, ada3, o_a, o_b, o_c, wg, wb, wo, tile):
    n = x.shape[0]
    seqs, tps = _tile_plan(t_len, tile)
    full = lambda a: pl.BlockSpec(a.shape, lambda i: (0,) * a.ndim)
    ada_spec = lambda j: pl.BlockSpec((seqs, 1, D_MODEL), lambda i: (i // tps, 0, j))
    row = lambda w: pl.BlockSpec((tile, w), lambda i: (i, 0))
    return pl.pallas_call(
        functools.partial(_merge_kernel, seqs),
        grid=(n // tile,),
        in_specs=[row(D_MODEL), full(g), ada_spec(0), ada_spec(1), ada_spec(2),
                  row(A_WIDTH), row(B_INNER), row(C_WIDTH), full(wg), full(wb), full(wo)],
        out_specs=row(D_MODEL),
        out_shape=jax.ShapeDtypeStruct((n, D_MODEL), F32),
        compiler_params=_params("arbitrary"),
        name="merge",
    )(x, g, ada3, ada3, ada3, o_a, o_b, o_c, wg, wb, wo)


def _route(scores, bias):
    lane = _iota2(scores.shape, 1)
    sel = scores + bias

    def partner(x, d, span):
        wrap = (lane & (span - 1)) + d >= span
        fwd = pltpu.roll(x, x.shape[1] - d, 1)
        back = pltpu.roll(x, span - d, 1)
        return jnp.where(wrap, back, fwd), wrap

    rank = jnp.zeros(scores.shape, jnp.int32)
    for d in range(1, EXPERTS_PER_GROUP):
        other, wrap = partner(sel, d, EXPERTS_PER_GROUP)
        beats = (other > sel) | (wrap & (other == sel))
        rank = rank + jnp.where(beats, 1, 0)
    in_top = rank < 2
    g_score = jnp.where(in_top, sel, 0.0)
    top_s = jnp.where(in_top, scores, 0.0)
    g_sum, w_sum = g_score, top_s
    for d in range(1, EXPERTS_PER_GROUP):
        g_sum = g_sum + partner(g_score, d, EXPERTS_PER_GROUP)[0]
        w_sum = w_sum + partner(top_s, d, EXPERTS_PER_GROUP)[0]
    chosen = in_top & (lane < N_EXPERTS)
    for d in range(EXPERTS_PER_GROUP, N_EXPERTS, EXPERTS_PER_GROUP):
        other, wrap = partner(g_sum, d, N_EXPERTS)
        chosen = chosen & ((other < g_sum) | (jnp.logical_not(wrap) & (other == g_sum)))
    return jnp.where(chosen, scores / w_sum, 0.0)


def _moe_kernel(seqs, final, x_ref, g_ref, sh_ref, sc_ref, gt_ref, wr_ref, br_ref, fg_ref,
                wg_ref, wu_ref, wd_ref, out_ref, hb_ref, coef_ref, acc_ref):
    e = pl.program_id(1)
    rows = x_ref.shape[0]

    @pl.when(e == 0)
    def _():
        h = _norm_mod(x_ref[...], g_ref[...], sh_ref[...], sc_ref[...], seqs)
        hb_ref[...] = h.astype(BF16)
        scores = _sigmoid(_mm(_nn, h, wr_ref[...], 1))
        coef_ref[...] = _route(scores, br_ref[...])
        acc_ref[...] = jnp.zeros_like(acc_ref)

    hb = hb_ref[...]
    coef = coef_ref[...]
    c_e = jnp.sum(jnp.where(_iota2(coef.shape, 1) == e, coef, 0.0), axis=-1, keepdims=True)
    hid = _silu(_nn(hb, wg_ref[...])) * _nn(hb, wu_ref[...])
    y = _nn(hid.astype(BF16), wd_ref[...])
    acc_ref[...] += jnp.where(c_e != 0.0, c_e * y, 0.0)

    @pl.when(e == pl.num_programs(1) - 1)
    def _():
        x = x_ref[...]
        out = x.reshape(seqs, rows // seqs, D_MODEL) + gt_ref[...] * acc_ref[...].reshape(seqs, rows // seqs, D_MODEL)
        out = out.reshape(rows, D_MODEL)
        if final:
            out = out * lax.rsqrt(jnp.mean(out * out, axis=-1, keepdims=True) + RMS_EPS) * fg_ref[...]
        out_ref[...] = out


def _moe(x, t_len, g, ada3, w_router, b_router, final_g, wg, wu, wd, tile, final):
    n = x.shape[0]
    seqs, tps = _tile_plan(t_len, tile)
    full = lambda a: pl.BlockSpec(a.shape, lambda i, e: (0,) * a.ndim)
    ada_spec = lambda j: pl.BlockSpec((seqs, 1, D_MODEL), lambda i, e: (i // tps, 0, j))
    row = pl.BlockSpec((tile, D_MODEL), lambda i, e: (i, 0))
    expert = lambda a: pl.BlockSpec((None,) + a.shape[1:], lambda i, e: (e, 0, 0))
    return pl.pallas_call(
        functools.partial(_moe_kernel, seqs, final),
        grid=(n // tile, N_EXPERTS),
        in_specs=[row, full(g), ada_spec(3), ada_spec(4), ada_spec(5), full(w_router), full(b_router),
                  full(final_g), expert(wg), expert(wu), expert(wd)],
        out_specs=row,
        out_shape=jax.ShapeDtypeStruct((n, D_MODEL), F32),
        scratch_shapes=[pltpu.VMEM((tile, D_MODEL), BF16),
                        pltpu.VMEM((tile, LANES), F32),
                        pltpu.VMEM((tile, D_MODEL), F32)],
        compiler_params=_params("arbitrary", "arbitrary"),
        name="moe",
    )(x, g, ada3, ada3, ada3, w_router, b_router, final_g, wg, wu, wd)


def _prep_weights(w_in):
    o_b = A_COLS
    o_c = A_COLS + B_COLS
    o_g = o_c + C_COLS
    wa = w_in[..., :o_b]
    wb = jnp.pad(w_in[..., o_b:o_c], ((0, 0), (0, 0), (0, B_DT_PAD - B_HEADS)))
    wc = _pad_lora_cols(w_in[..., o_c:o_g])
    wgate = w_in[..., o_g:]
    return tuple(w.astype(BF16) for w in (wa, wb, wc, wgate))


def _trunk(x, ada, states, weights, tiles):
    nb, t_len, _ = x.shape
    st_hgrn, st_ssm, st_conv, st_wkv, st_shift = states
    (wa, wb, wc, wgate, w_branch, w_out, w_gate_e, w_up_e, w_down_e, w_router, b_router, p) = weights
    tile_tok, tile_moe = tiles
    depth = wa.shape[0]
    xf = x.reshape(nb * t_len, D_MODEL)
    row = lambda v: v.reshape(1, -1)
    outs = []
    for l in range(depth):
        ada3 = ada[l].reshape(nb, 1, -1)
        g1 = row(p['norm1_g'][l])
        cols_a, cols_b, cols_c = _inproj(xf, t_len, g1, ada3, wa[l], wb[l], wc[l], tile_tok)
        o_a, hgrn_new = _hgrn(cols_a, nb, t_len, l, p['hgrn_lb'], row(p['hgrn_norm_g'][l]), st_hgrn[l])
        o_b, ssm_new = _ssd(cols_b, nb, t_len, p['ssm_conv_w'][l], p['ssm_conv_b'][l], p['ssm_dt_bias'][l],
                            p['ssm_a_log'][l], p['ssm_d'][l], p['ssm_norm_g'][l], st_conv[l], st_ssm[l])
        o_c, wkv_new = _rwkv(cols_c, nb, t_len, st_shift[l], st_wkv[l], p['rwkv_mu'][l], p['rwkv_w0'][l],
                             p['rwkv_w_w2'][l], p['rwkv_a0'][l], p['rwkv_a_w2'][l], p['rwkv_g_w2'][l],
                             p['rwkv_k_k'][l], p['rwkv_k_a'][l], p['rwkv_r_k'][l].reshape(-1),
                             p['rwkv_ln_w'][l], p['rwkv_ln_b'][l])
        assert t_len >= B_CONV - 1
        conv_new = cols_b.reshape(nb, t_len, B_COLS_PAD)[:, t_len - (B_CONV - 1):, B_INNER:B_INNER + B_CONV_DIM]
        shift_new = _unpad_lora_cols(cols_c.reshape(nb, t_len, C_COLS_PAD)[:, t_len - 1:])
        xf = _merge(xf, t_len, g1, ada3, o_a, o_b, o_c, wgate[l], w_branch[l], w_out[l], tile_tok)
        xf = _moe(xf, t_len, row(p['norm2_g'][l]), ada3, w_router, b_router, row(p['final_g']),
                  w_gate_e[l], w_up_e[l], w_down_e[l], tile_moe, final=(l == depth - 1))
        outs.append((hgrn_new, ssm_new, conv_new, wkv_new, shift_new))
    new_states = [jnp.stack(s, axis=0) for s in zip(*outs)]
    return xf.reshape(nb, t_len, D_MODEL), new_states


def kernel(x_prompt, x_sample, c_prompt, c_sample, state_hgrn, state_ssm, state_conv, state_wkv, state_shift,
           w_ada, b_ada, norm1_g, norm2_g, final_g, w_in, hgrn_lb, hgrn_norm_g,
           ssm_conv_w, ssm_conv_b, ssm_dt_bias, ssm_a_log, ssm_d, ssm_norm_g,
           rwkv_mu, rwkv_w0, rwkv_w_w2, rwkv_a0, rwkv_a_w2, rwkv_g_w2, rwkv_k_k, rwkv_k_a, rwkv_r_k,
           rwkv_ln_w, rwkv_ln_b, w_branch, w_out, w_router, b_router, w_gate_e, w_up_e, w_down_e):
    p = {'norm1_g': norm1_g, 'norm2_g': norm2_g, 'final_g': final_g, 'hgrn_lb': hgrn_lb,
         'hgrn_norm_g': hgrn_norm_g, 'ssm_conv_w': ssm_conv_w, 'ssm_conv_b': ssm_conv_b,
         'ssm_dt_bias': ssm_dt_bias, 'ssm_a_log': ssm_a_log, 'ssm_d': ssm_d, 'ssm_norm_g': ssm_norm_g,
         'rwkv_mu': rwkv_mu, 'rwkv_w0': rwkv_w0, 'rwkv_w_w2': rwkv_w_w2, 'rwkv_a0': rwkv_a0,
         'rwkv_a_w2': rwkv_a_w2, 'rwkv_g_w2': rwkv_g_w2, 'rwkv_k_k': rwkv_k_k, 'rwkv_k_a': rwkv_k_a,
         'rwkv_r_k': rwkv_r_k, 'rwkv_ln_w': rwkv_ln_w, 'rwkv_ln_b': rwkv_ln_b}
    depth = w_in.shape[0]
    bp, tp = x_prompt.shape[0], x_prompt.shape[1]
    bs, ts = x_sample.shape[0], x_sample.shape[1]
    wa, wb, wc, wgate = _prep_weights(w_in)
    w_router_pad = jnp.pad(w_router, ((0, 0), (0, LANES - N_EXPERTS)))
    b_router_pad = jnp.pad(b_router.reshape(1, -1), ((0, 0), (0, LANES - N_EXPERTS)))
    weights = (wa, wb, wc, wgate, w_branch.astype(BF16), w_out.astype(BF16),
               w_gate_e.astype(BF16), w_up_e.astype(BF16), w_down_e.astype(BF16),
               w_router_pad, b_router_pad, p)
    ada = _ada(jnp.concatenate([c_prompt, c_sample], axis=0), w_ada, b_ada)

    zeros_p = (jnp.zeros((depth, bp, A_HEADS, A_DK, A_DV), F32),
               jnp.zeros((depth, bp, B_HEADS, B_HEADDIM, B_STATE), F32),
               jnp.zeros((depth, bp, B_CONV - 1, B_CONV_DIM), x_prompt.dtype),
               jnp.zeros((depth, bp, C_HEADS, C_HEADDIM, C_HEADDIM), F32),
               jnp.zeros((depth, bp, 1, C_COLS), x_prompt.dtype))
    tile_p = min(256, tp)
    y_prompt, st_p = _trunk(x_prompt, ada[:, :bp], zeros_p, weights, (tile_p, min(1024, tp)))
    tile_s = min(256, bs * ts)
    y_sample, st_s = _trunk(x_sample, ada[:, bp:], (state_hgrn, state_ssm, state_conv, state_wkv, state_shift),
                            weights, (tile_s, min(512, bs * ts)))
    return (y_prompt, y_sample, *st_p, *st_s)
```

```python
import functools

import jax
import jax.numpy as jnp
from jax import lax
from jax.experimental import pallas as pl
from jax.experimental.pallas import tpu as pltpu

F32 = jnp.float32
BF16 = jnp.bfloat16

D_MODEL = 1024
RMS_EPS = 1e-6
A_HEADS = 4
A_DK = 128
A_DV = 128
A_WIDTH = A_HEADS * A_DV
A_COLS = 4 * A_WIDTH
LB_FLOOR = 1e-30
B_HEADS = 8
B_HEADDIM = 64
B_INNER = B_HEADS * B_HEADDIM
B_GROUPS = 2
B_HPG = B_HEADS // B_GROUPS
B_STATE = 128
B_CONV = 4
B_CONV_DIM = B_INNER + 2 * B_GROUPS * B_STATE
B_COLS = B_INNER + B_CONV_DIM + B_HEADS
C_HEADS = 8
C_HEADDIM = 64
C_WIDTH = C_HEADS * C_HEADDIM
C_DECAY_LORA = 64
C_ICLR_LORA = 64
C_GATE_LORA = 128
C_COLS = 3 * C_WIDTH + C_DECAY_LORA + C_ICLR_LORA + C_GATE_LORA
C_LN_EPS = 64e-5
N_BRANCH = 3
GATE_COLS = N_BRANCH * D_MODEL
MIX_WIDTH = A_WIDTH + B_INNER + C_WIDTH
N_EXPERTS = 16
N_EXPERT_GROUPS = 4
EXPERTS_PER_GROUP = N_EXPERTS // N_EXPERT_GROUPS
D_FF_EXPERT = 512

LANES = 128
SUBLANES = 8
VMEM_LIMIT = 56 * 1024 * 1024

B_DT_PAD = LANES
B_COLS_PAD = B_INNER + B_CONV_DIM + B_DT_PAD
C_LORA_PAD = LANES
C_COLS_PAD = 3 * C_WIDTH + 3 * C_LORA_PAD

TIME_BLOCK = 64


def _nn(a, b):
    return lax.dot_general(a, b, (((1,), (0,)), ((), ())), preferred_element_type=F32)


def _nt(a, b):
    return lax.dot_general(a, b, (((1,), (1,)), ((), ())), preferred_element_type=F32)


def _tn(a, b):
    return lax.dot_general(a, b, (((0,), (0,)), ((), ())), preferred_element_type=F32)


def _split2(x):
    hi = x.astype(BF16)
    lo = (x - hi.astype(F32)).astype(BF16)
    return hi, lo


def _split3(x):
    hi = x.astype(BF16)
    r1 = x - hi.astype(F32)
    mid = r1.astype(BF16)
    lo = (r1 - mid.astype(F32)).astype(BF16)
    return hi, mid, lo


def _mm(dot, a, b, passes):
    if passes == 1:
        return dot(a.astype(BF16), b.astype(BF16))
    a_hi, a_lo = _split2(a)
    b_hi, b_lo = _split2(b)
    return dot(a_hi, b_hi) + (dot(a_hi, b_lo) + dot(a_lo, b_hi))


def _exact_lhs_nn(m_bf16, x):
    hi, mid, lo = _split3(x)
    return _nn(m_bf16, hi) + (_nn(m_bf16, mid) + _nn(m_bf16, lo))


def _exact_rhs_nn(x, m_bf16):
    hi, mid, lo = _split3(x)
    return _nn(hi, m_bf16) + (_nn(mid, m_bf16) + _nn(lo, m_bf16))


def _sigmoid(x):
    return 1.0 / (1.0 + jnp.exp(-x))


def _silu(x):
    return x * _sigmoid(x)


def _softplus(x):
    return jnp.maximum(x, 0.0) + jnp.log1p(jnp.exp(-jnp.abs(x)))


def _iota2(shape, dim):
    return lax.broadcasted_iota(jnp.int32, shape, dim)


def _tril_bf16(n):
    return jnp.where(_iota2((n, n), 1) <= _iota2((n, n), 0), 1.0, 0.0).astype(BF16)


def _norm_mod(x, g, shift, scale, seqs):
    rows, d = x.shape
    y = x * lax.rsqrt(jnp.mean(x * x, axis=-1, keepdims=True) + RMS_EPS) * g
    y = y.reshape(seqs, rows // seqs, d) * (1.0 + scale) + shift
    return y.reshape(rows, d)


def _tile_plan(t_len, tile):
    if tile >= t_len:
        assert tile % t_len == 0
        return tile // t_len, 1
    assert t_len % tile == 0
    return 1, t_len // tile


def _seqs_per_step(nb):
    return 2 if nb % 2 == 0 else 1


def _params(*sem):
    return pltpu.CompilerParams(dimension_semantics=sem, vmem_limit_bytes=VMEM_LIMIT)


def _ada_kernel(c_ref, w_ref, b_ref, o_ref):
    s = _silu(c_ref[...])
    o_ref[...] = _mm(_nn, s, w_ref[...], 1) + b_ref[...]


def _ada(c, w_ada, b_ada):
    nb = c.shape[0]
    depth = w_ada.shape[0]
    n_blk = w_ada.shape[2] // D_MODEL
    return pl.pallas_call(
        _ada_kernel,
        grid=(depth, n_blk),
        in_specs=[pl.BlockSpec((nb, D_MODEL), lambda l, j: (0, 0)),
                  pl.BlockSpec((None, D_MODEL, D_MODEL), lambda l, j: (l, 0, j)),
                  pl.BlockSpec((None, 1, D_MODEL), lambda l, j: (l, 0, j))],
        out_specs=pl.BlockSpec((None, nb, D_MODEL), lambda l, j: (l, 0, j)),
        out_shape=jax.ShapeDtypeStruct((depth, nb, w_ada.shape[2]), F32),
        compiler_params=_params("arbitrary", "arbitrary"),
        name="ada",
    )(c, w_ada, b_ada.reshape(depth, 1, -1))


def _inproj_kernel(seqs, x_ref, g_ref, sh_ref, sc_ref, wa_ref, wb_ref, wc_ref, oa_ref, ob_ref, oc_ref):
    h = _norm_mod(x_ref[...], g_ref[...], sh_ref[...], sc_ref[...], seqs).astype(BF16)
    oa_ref[...] = _nn(h, wa_ref[...])
    ob_ref[...] = _nn(h, wb_ref[...])
    oc_ref[...] = _nn(h, wc_ref[...])


def _inproj(x, t_len, g, ada3, wa, wb, wc, tile):
    n = x.shape[0]
    seqs, tps = _tile_plan(t_len, tile)
    full = lambda a: pl.BlockSpec(a.shape, lambda i: (0,) * a.ndim)
    ada_spec = lambda j: pl.BlockSpec((seqs, 1, D_MODEL), lambda i: (i // tps, 0, j))
    row = lambda w: pl.BlockSpec((tile, w), lambda i: (i, 0))
    return pl.pallas_call(
        functools.partial(_inproj_kernel, seqs),
        grid=(n // tile,),
        in_specs=[row(D_MODEL), full(g), ada_spec(0), ada_spec(1), full(wa), full(wb), full(wc)],
        out_specs=[row(A_COLS), row(B_COLS_PAD), row(C_COLS_PAD)],
        out_shape=[jax.ShapeDtypeStruct((n, A_COLS), F32),
                   jax.ShapeDtypeStruct((n, B_COLS_PAD), F32),
                   jax.ShapeDtypeStruct((n, C_COLS_PAD), F32)],
        compiler_params=_params("arbitrary"),
        name="inproj",
    )(x, g, ada3, ada3, wa, wb, wc)


def _hgrn_kernel(layer, cols_ref, lb_ref, ng_ref, s0_ref, o_ref, sout_ref, st_ref, attn_ref):
    t = pl.program_id(1)
    nseq, tb = cols_ref.shape[0], cols_ref.shape[1]
    unit = [(s, h) for s in range(nseq) for h in range(A_HEADS)]

    @pl.when(t == 0)
    def _():
        for s, h in unit:
            st_ref[s, h] = s0_ref[s, h].T

    rows = [lb_ref[i:i + 1, :] for i in range(lb_ref.shape[0])]
    mx = functools.reduce(jnp.maximum, rows)
    es = [jnp.exp(r - mx) for r in rows]
    den = functools.reduce(lambda a, b: a + b, es)
    lower = jnp.zeros_like(mx)
    for i in range(1, layer + 1):
        lower = lower + es[i] / den

    log_lower = jnp.log(jnp.maximum(lower, LB_FLOOR))
    log1m_lower = jnp.log1p(-lower)
    tril = _tril_bf16(tb)

    def prep(s):
        f_pre = cols_ref[s, :, A_WIDTH:2 * A_WIDTH]
        log_sig = jnp.minimum(f_pre, 0.0) - jnp.log1p(jnp.exp(-jnp.abs(f_pre)))
        bterm = log1m_lower + log_sig
        log_f = jnp.maximum(log_lower, bterm) + jnp.log1p(jnp.exp(-jnp.abs(log_lower - bterm)))
        return dict(q=cols_ref[s, :, 0:A_WIDTH], k=(1.0 - lower) / (1.0 + jnp.exp(f_pre)),
                    v=cols_ref[s, :, 2 * A_WIDTH:3 * A_WIDTH].astype(BF16),
                    b=_exact_lhs_nn(tril, log_f))

    seq = [prep(s) for s in range(nseq)]
    units = range(len(unit))
    col = lambda name: [seq[s][name][:, h * A_DK:(h + 1) * A_DK] for s, h in unit]
    q, kf, vh, bh = col('q'), col('k'), col('v'), col('b')
    st = [st_ref[s, h] for s, h in unit]

    blk = SUBLANES
    nblk = tb // blk
    rowblk = _iota2((tb, 1), 0) >> (blk.bit_length() - 1)
    attn_ref[...] = jnp.zeros_like(attn_ref)
    att_off = []
    for u in units:
        k2 = jnp.concatenate([kf[u][j * blk:(j + 1) * blk]
                              * jnp.exp(bh[u][(j + 1) * blk - 1:(j + 1) * blk] - bh[u][j * blk:(j + 1) * blk])
                              for j in range(nblk)], axis=0)
        acc = None
        for j in range(nblk - 1):
            r0 = (j + 1) * blk
            qe = q[u][r0:] * jnp.exp(bh[u][r0:] - bh[u][r0 - 1:r0])
            part = _mm(_nt, qe, jnp.where(rowblk == j, k2, 0.0), 3)
            part = jnp.concatenate([jnp.zeros((r0, tb), F32), part], axis=0)
            acc = part if acc is None else acc + part
        att_off.append(acc)
        s_, h_ = unit[u]
        for sc in range(tb):
            j0 = (sc // blk) * blk
            live = _iota2((blk, 1), 0) >= sc - j0
            diff = bh[u][j0:j0 + blk] - bh[u][sc:sc + 1]
            dec = jnp.where(live, jnp.exp(jnp.where(live, diff, 0.0)), 0.0)
            attn_ref[s_, h_, j0:j0 + blk, sc:sc + 1] = jnp.sum(
                q[u][j0:j0 + blk] * kf[u][sc:sc + 1] * dec, axis=-1, keepdims=True)

    g_all = [cols_ref[s, :, 3 * A_WIDTH:4 * A_WIDTH] for s in range(nseq)]
    for u in units:
        s_, h_ = unit[u]
        hs = slice(h_ * A_DV, (h_ + 1) * A_DV)
        att = attn_ref[s_, h_] + att_off[u]
        oh = (_nt((q[u] * jnp.exp(bh[u])).astype(BF16), st[u].astype(BF16))
              + _nn(att.astype(BF16), vh[u]))
        b_last = bh[u][tb - 1:tb]
        kdec = kf[u] * jnp.exp(b_last - bh[u])
        st_ref[s_, h_] = st[u] * jnp.exp(b_last) + _tn(vh[u], kdec.astype(BF16))
        on = oh * lax.rsqrt(jnp.mean(oh * oh, axis=-1, keepdims=True) + RMS_EPS) * ng_ref[:, hs]
        o_ref[s_, :, hs] = (on * _silu(g_all[s_][:, hs])).astype(o_ref.dtype)

    @pl.when(t == pl.num_programs(1) - 1)
    def _():
        for s, h in unit:
            sout_ref[s, h] = st_ref[s, h].T


def _hgrn(cols_a, nb, t_len, layer, hgrn_lb, norm_g, state0):
    tb = min(TIME_BLOCK, t_len)
    nt = t_len // tb
    nseq = _seqs_per_step(nb)
    state_spec = pl.BlockSpec((nseq, A_HEADS, A_DK, A_DV), lambda b, t: (b, 0, 0, 0))
    o, s_out = pl.pallas_call(
        functools.partial(_hgrn_kernel, layer),
        grid=(nb // nseq, nt),
        in_specs=[pl.BlockSpec((nseq, tb, A_COLS), lambda b, t: (b, t, 0)),
                  pl.BlockSpec(hgrn_lb.shape, lambda b, t: (0, 0)),
                  pl.BlockSpec((1, A_WIDTH), lambda b, t: (0, 0)),
                  state_spec],
        out_specs=[pl.BlockSpec((nseq, tb, A_WIDTH), lambda b, t: (b, t, 0)), state_spec],
        out_shape=[jax.ShapeDtypeStruct((nb, t_len, A_WIDTH), BF16),
                   jax.ShapeDtypeStruct((nb, A_HEADS, A_DK, A_DV), F32)],
        scratch_shapes=[pltpu.VMEM((nseq, A_HEADS, A_DV, A_DK), F32),
                        pltpu.VMEM((nseq, A_HEADS, tb, tb), F32)],
        compiler_params=_params("arbitrary", "arbitrary"),
        name="hgrn2",
    )(cols_a.reshape(nb, t_len, A_COLS), hgrn_lb, norm_g, state0)
    return o.reshape(nb * t_len, A_WIDTH), s_out


def _ssd_kernel(cols_ref, cw_ref, cb_ref, dtb_ref, alog_ref, dsk_ref, ng_ref, cs0_ref, s0_ref,
                o_ref, sout_ref, ext_ref, st_ref):
    t = pl.program_id(1)
    nseq, tb = cols_ref.shape[0], cols_ref.shape[1]
    n_pair = B_HEADS // 2
    pair_w = 2 * B_HEADDIM

    @pl.when(t == 0)
    def _():
        ext_ref[:, 0:SUBLANES, :] = jnp.zeros((nseq, SUBLANES, B_CONV_DIM), F32)
        ext_ref[:, SUBLANES - (B_CONV - 1):SUBLANES, :] = cs0_ref[...]
        st_ref[...] = s0_ref[...]

    rnd = lambda a: a.astype(BF16).astype(F32)
    tril = _tril_bf16(tb)
    causal = _iota2((tb, tb), 1) <= _iota2((tb, tb), 0)
    lane_lo = _iota2((1, pair_w), 1) < B_HEADDIM
    row_lo = _iota2((pair_w, 1), 0) < B_HEADDIM
    neg_a = -jnp.exp(alog_ref[...])
    states = [[st_ref[s, p] for p in range(n_pair)] for s in range(nseq)]
    new_states = []

    for s in range(nseq):
        z = cols_ref[s, :, 0:B_INNER]
        ext_ref[s, SUBLANES:SUBLANES + tb, :] = cols_ref[s, :, B_INNER:B_INNER + B_CONV_DIM]
        conv = None
        for j in range(B_CONV):
            tap = rnd(cw_ref[j:j + 1, :]) * rnd(ext_ref[s, pl.ds(SUBLANES - (B_CONV - 1) + j, tb), :])
            conv = tap if conv is None else conv + tap
        conv = conv + cb_ref[...]
        ext_ref[s, 0:SUBLANES, :] = ext_ref[s, tb:tb + SUBLANES, :]
        xbc = _silu(conv)
        xs = xbc[:, 0:B_INNER]
        bm = xbc[:, B_INNER:B_INNER + B_GROUPS * B_STATE].astype(BF16)
        cm = xbc[:, B_INNER + B_GROUPS * B_STATE:].astype(BF16)

        dt = _softplus(cols_ref[s, :, B_INNER + B_CONV_DIM:] + dtb_ref[...])
        a = _exact_lhs_nn(tril, dt * neg_a)
        a_t = a.T
        dt_t = dt.T
        cb = [_nt(cm[:, g * B_STATE:(g + 1) * B_STATE], bm[:, g * B_STATE:(g + 1) * B_STATE])
              for g in range(B_GROUPS)]

        y_pairs = []
        for p in range(n_pair):
            grp = (2 * p) // B_HPG
            bg = bm[:, grp * B_STATE:(grp + 1) * B_STATE]
            cg = cm[:, grp * B_STATE:(grp + 1) * B_STATE]
            xp = xs[:, p * pair_w:(p + 1) * pair_w]
            sp = states[s][p]
            y_p = _nt(cg, sp.astype(BF16))
            a_cols, w_cols, e_last = [], [], []
            for r in range(2):
                hh = 2 * p + r
                a_col = a[:, hh:hh + 1]
                a_last = a[tb - 1:tb, hh:hh + 1]
                a_cols.append(a_col)
                w_cols.append(jnp.exp(a_last - a_col) * dt[:, hh:hh + 1])
                e_last.append(jnp.exp(a_last))
            y_p = y_p * jnp.where(lane_lo, jnp.exp(a_cols[0]), jnp.exp(a_cols[1]))
            for r in range(2):
                hh = 2 * p + r
                seg = jnp.where(causal, jnp.exp(jnp.where(causal, a_cols[r] - a_t[hh:hh + 1, :], 0.0)), 0.0)
                m = cb[grp] * seg * dt_t[hh:hh + 1, :]
                half = lane_lo if r == 0 else jnp.logical_not(lane_lo)
                y_p = y_p + _nn(m.astype(BF16), jnp.where(half, xp, 0.0).astype(BF16))
            xw = xp * jnp.where(lane_lo, w_cols[0], w_cols[1])
            new_states.append(sp * jnp.where(row_lo, e_last[0], e_last[1]) + _tn(xw.astype(BF16), bg))
            y_pairs.append(y_p)

        y = jnp.concatenate(y_pairs, axis=-1) + dsk_ref[...] * xs
        yz = y * _silu(z)
        o_ref[s] = (yz * lax.rsqrt(jnp.mean(yz * yz, axis=-1, keepdims=True) + RMS_EPS)
                    * ng_ref[...]).astype(o_ref.dtype)

    for s in range(nseq):
        for p in range(n_pair):
            st_ref[s, p] = new_states[s * n_pair + p]

    @pl.when(t == pl.num_programs(1) - 1)
    def _():
        sout_ref[...] = st_ref[...]


def _ssd(cols_b, nb, t_len, conv_w, conv_b, dt_bias, a_log, d_skip, norm_g, conv_state, ssm_state):
    tb = min(TIME_BLOCK, t_len)
    nt = t_len // tb
    n_pair = B_HEADS // 2
    pair_w = 2 * B_HEADDIM
    pad8 = lambda v: jnp.pad(v.reshape(1, B_HEADS), ((0, 0), (0, B_DT_PAD - B_HEADS)))
    const = lambda a: pl.BlockSpec(a.shape, lambda b, t: (0,) * a.ndim)
    args = (conv_w, conv_b.reshape(1, -1), pad8(dt_bias), pad8(a_log),
            jnp.repeat(d_skip, B_HEADDIM).reshape(1, B_INNER), norm_g.reshape(1, -1))
    nseq = _seqs_per_step(nb)
    state_spec = pl.BlockSpec((nseq, n_pair, pair_w, B_STATE), lambda b, t: (b, 0, 0, 0))
    o, s_out = pl.pallas_call(
        _ssd_kernel,
        grid=(nb // nseq, nt),
        in_specs=[pl.BlockSpec((nseq, tb, B_COLS_PAD), lambda b, t: (b, t, 0))]
                 + [const(a) for a in args]
                 + [pl.BlockSpec((nseq, B_CONV - 1, B_CONV_DIM), lambda b, t: (b, 0, 0)), state_spec],
        out_specs=[pl.BlockSpec((nseq, tb, B_INNER), lambda b, t: (b, t, 0)), state_spec],
        out_shape=[jax.ShapeDtypeStruct((nb, t_len, B_INNER), BF16),
                   jax.ShapeDtypeStruct((nb, n_pair, pair_w, B_STATE), F32)],
        scratch_shapes=[pltpu.VMEM((nseq, tb + SUBLANES, B_CONV_DIM), F32),
                        pltpu.VMEM((nseq, n_pair, pair_w, B_STATE), F32)],
        compiler_params=_params("arbitrary", "arbitrary"),
        name="ssd",
    )(cols_b.reshape(nb, t_len, B_COLS_PAD), *args, conv_state, ssm_state.reshape(nb, n_pair, pair_w, B_STATE))
    return o.reshape(nb * t_len, B_INNER), s_out.reshape(nb, B_HEADS, B_HEADDIM, B_STATE)


RWKV_PASSES = 3


def _rwkv_kernel(cols_ref, mu_ref, w0_ref, ww2_ref, a0_ref, aw2_ref, gw2_ref, kk_ref, ka_ref, rk_ref,
                 lnw_ref, lnb_ref, sh0_ref, s0_ref, o_ref, sout_ref, ext_ref, st_ref):
    t = pl.program_id(1)
    nseq, tb = cols_ref.shape[0], cols_ref.shape[1]
    n_pair = C_HEADS // 2
    pw = 2 * C_HEADDIM
    s3 = 3 * C_WIDTH
    mm = functools.partial(_mm, passes=RWKV_PASSES)

    @pl.when(t == 0)
    def _():
        ext_ref[:, 0:SUBLANES, :] = jnp.zeros((nseq, SUBLANES, C_COLS_PAD), F32)
        ext_ref[:, SUBLANES - 1:SUBLANES, :] = sh0_ref[...]
        st_ref[...] = s0_ref[...]

    head_of = lambda n, d: _iota2((n, n), d) >> (C_HEADDIM.bit_length() - 1)
    seg_ones = jnp.where(head_of(C_WIDTH, 0) == head_of(C_WIDTH, 1), 1.0, 0.0).astype(BF16)

    def segsum(x):
        hi, lo = _split2(x)
        return _nn(hi, seg_ones) + _nn(lo, seg_ones)

    tril = _tril_bf16(tb)
    rnd = lambda a: a.astype(BF16).astype(F32)

    def prep(s):
        cur = cols_ref[s]
        ext_ref[s, SUBLANES:SUBLANES + tb, :] = cur
        prev = ext_ref[s, pl.ds(SUBLANES - 1, tb), :]
        ext_ref[s, 0:SUBLANES, :] = ext_ref[s, tb:tb + SUBLANES, :]
        mixed = cur + (prev - cur) * mu_ref[...]
        r = mixed[:, 0:C_WIDTH]
        k = mixed[:, C_WIDTH:2 * C_WIDTH]
        v = mixed[:, 2 * C_WIDTH:s3]
        w_lo = mixed[:, s3:s3 + C_LORA_PAD]
        a_lo = mixed[:, s3 + C_LORA_PAD:s3 + 2 * C_LORA_PAD]
        g_lo = mixed[:, s3 + 2 * C_LORA_PAD:]
        w_raw = -_softplus(-(w0_ref[...] + _mm(_nn, jnp.tanh(w_lo), ww2_ref[...], 1))) - 0.5
        logw = -jnp.exp(w_raw)
        a = _sigmoid(a0_ref[...] + _mm(_nn, a_lo, aw2_ref[...], 1))
        g = _mm(_nn, _sigmoid(g_lo), gw2_ref[...], 1)
        kk = k * kk_ref[...]
        kk = kk / jnp.maximum(jnp.sqrt(segsum(kk * kk)), 1e-12)
        k2 = k * (1.0 + (a - 1.0) * ka_ref[...])
        beta = kk * a
        c = _exact_lhs_nn(tril, logw)
        c_last = c[tb - 1:tb, :]
        g_inv = jnp.exp(-c)
        g_hat = jnp.exp(c_last - c)
        return dict(r=r, v=v, g=g, k2=k2, gam_last=jnp.exp(c_last),
                    ab=-rnd(kk) * jnp.exp(c - logw), rb=rnd(r) * jnp.exp(c),
                    bt=beta * g_inv, kt=k2 * g_inv, bh=beta * g_hat, kh=k2 * g_hat)

    seq = [prep(s) for s in range(nseq)]

    n2 = 2 * tb
    ri = _iota2((n2, n2), 0)
    ci = _iota2((n2, n2), 1)
    keep = jnp.where(ri < tb, ci & (tb - 1), (ci & (tb - 1)) - 1) < (ri & (tb - 1))
    eye = jnp.where(ri == ci, 1.0, 0.0)
    left_h = _iota2((tb, n2), 1) < tb
    lane0 = _iota2((1, pw), 1) < C_HEADDIM
    rows_top = _iota2((n2, 1), 0) < tb
    d_own = (rows_top & lane0) | (jnp.logical_not(rows_top) & jnp.logical_not(lane0))
    blockdiag = (_iota2((pw, pw), 0) < C_HEADDIM) == (_iota2((pw, pw), 1) < C_HEADDIM)
    stack = lambda a, b: jnp.concatenate([a, b], axis=0)
    unit = [(s, p) for s in range(nseq) for p in range(n_pair)]
    pairs = range(len(unit))
    col = lambda name: [seq[s][name][:, p * pw:(p + 1) * pw] for s, p in unit]
    sp = [st_ref[s, p] for s, p in unit]
    vp, ab, rb, bt, kt = col('v'), col('ab'), col('rb'), col('bt'), col('kt')
    x = [stack(ab[p], rb[p]) for p in pairs]
    yk = [jnp.concatenate([jnp.where(lane0, bt[p], 0.0), jnp.where(lane0, kt[p], 0.0),
                           jnp.where(lane0, 0.0, kt[p]), jnp.where(lane0, 0.0, bt[p])], axis=0)
          for p in pairs]
    pp = [mm(_nt, x[p], yk[p]) for p in pairs]
    blk0 = [jnp.where(keep, pp[p][:, 0:n2], 0.0) for p in pairs]
    blk1 = [jnp.where(keep, pp[p][:, n2:2 * n2], 0.0) for p in pairs]
    a_bd = [stack(jnp.where(left_h, blk0[p][0:tb], 0.0), jnp.where(left_h, 0.0, blk1[p][0:tb])) for p in pairs]
    q_ak = [stack(jnp.where(left_h, 0.0, blk0[p][0:tb]), jnp.where(left_h, blk1[p][0:tb], 0.0)) for p in pairs]
    r_mat = [stack(blk1[p][tb:n2], blk0[p][tb:n2]) for p in pairs]
    gs = [mm(_nt, x[p], sp[p]) for p in pairs]
    w = [jnp.where(d_own, stack(gs[p][0:tb], gs[p][0:tb]) + mm(_nn, q_ak[p], stack(vp[p], vp[p])), 0.0)
         for p in pairs]
    t_inv = [eye + a_bd[p] for p in pairs]
    a_pow = list(a_bd)
    for _ in range(max(1, (tb - 1).bit_length()) - 1):
        a_pow = [_mm(_nn, a_pow[p], a_pow[p], 1) for p in pairs]
        t_inv = [t_inv[p] + _mm(_nn, t_inv[p], a_pow[p], 1) for p in pairs]
    u = [_mm(_nn, t_inv[p], w[p], 1) for p in pairs]
    res = [w[p] - u[p] + mm(_nn, a_bd[p], u[p]) for p in pairs]
    u = [u[p] + _mm(_nn, t_inv[p], res[p], 1) for p in pairs]
    zz = [u[p] + stack(jnp.where(lane0, 0.0, vp[p]), jnp.where(lane0, vp[p], 0.0)) for p in pairs]
    yf = [jnp.where(d_own, 0.0, mm(_nn, r_mat[p], zz[p]) + stack(gs[p][tb:n2], gs[p][tb:n2])) for p in pairs]
    y_pairs = [yf[p][0:tb] + yf[p][tb:n2] for p in pairs]
    uv = [stack(u[p][0:tb] + u[p][tb:n2], vp[p]) for p in pairs]
    bh, kh, gam = col('bh'), col('kh'), col('gam_last')
    s_new = [sp[p] * gam[p] + jnp.where(blockdiag, mm(_tn, uv[p], stack(bh[p], kh[p])), 0.0) for p in pairs]
    for i, (s, p) in enumerate(unit):
        st_ref[s, p] = s_new[i]

    inv_n = 1.0 / C_HEADDIM
    for s in range(nseq):
        d = seq[s]
        y = jnp.concatenate(y_pairs[s * n_pair:(s + 1) * n_pair], axis=-1)
        mean = segsum(y) * inv_n
        dlt = y - mean
        var = segsum(dlt * dlt) * inv_n
        yn = dlt * lax.rsqrt(var + C_LN_EPS) * lnw_ref[...] + lnb_ref[...]
        bonus = segsum(d['r'] * d['k2'] * rk_ref[...]) * d['v']
        o_ref[s] = ((yn + bonus) * d['g']).astype(o_ref.dtype)

    @pl.when(t == pl.num_programs(1) - 1)
    def _():
        sout_ref[...] = st_ref[...]


def _pad_lora_cols(a):
    s3 = 3 * C_WIDTH
    z = jnp.zeros(a.shape[:-1] + (C_LORA_PAD - C_DECAY_LORA,), a.dtype)
    return jnp.concatenate([a[..., :s3 + C_DECAY_LORA], z,
                            a[..., s3 + C_DECAY_LORA:s3 + C_DECAY_LORA + C_ICLR_LORA], z,
                            a[..., s3 + C_DECAY_LORA + C_ICLR_LORA:]], axis=-1)


def _unpad_lora_cols(a):
    s3 = 3 * C_WIDTH
    return jnp.concatenate([a[..., :s3 + C_DECAY_LORA],
                            a[..., s3 + C_LORA_PAD:s3 + C_LORA_PAD + C_ICLR_LORA],
                            a[..., s3 + 2 * C_LORA_PAD:]], axis=-1)


def _rwkv(cols_c, nb, t_len, shift_state, wkv_state, mu, w0, w_w2, a0, a_w2, g_w2, k_k, k_a, r_k, ln_w, ln_b):
    tb = min(TIME_BLOCK, t_len)
    nt = t_len // tb
    n_pair = C_HEADS // 2
    pw = 2 * C_HEADDIM
    row = lambda v: v.reshape(1, -1)
    pad_rows = lambda w: jnp.pad(w, ((0, C_LORA_PAD - w.shape[0]), (0, 0)))
    args = (row(_pad_lora_cols(mu)), row(w0), pad_rows(w_w2), row(a0), pad_rows(a_w2), g_w2,
            row(k_k), row(k_a), row(r_k), row(ln_w), row(ln_b))
    const = lambda a: pl.BlockSpec(a.shape, lambda b, t: (0,) * a.ndim)
    s4 = wkv_state.reshape(nb, n_pair, 2, C_HEADDIM, C_HEADDIM)
    zero = jnp.zeros_like(s4[:, :, 0])
    s_bd = jnp.concatenate([jnp.concatenate([s4[:, :, 0], zero], axis=-1),
                            jnp.concatenate([zero, s4[:, :, 1]], axis=-1)], axis=-2)
    nseq = _seqs_per_step(nb)
    state_spec = pl.BlockSpec((nseq, n_pair, pw, pw), lambda b, t: (b, 0, 0, 0))
    o, s_out = pl.pallas_call(
        _rwkv_kernel,
        grid=(nb // nseq, nt),
        in_specs=[pl.BlockSpec((nseq, tb, C_COLS_PAD), lambda b, t: (b, t, 0))]
                 + [const(a) for a in args]
                 + [pl.BlockSpec((nseq, 1, C_COLS_PAD), lambda b, t: (b, 0, 0)), state_spec],
        out_specs=[pl.BlockSpec((nseq, tb, C_WIDTH), lambda b, t: (b, t, 0)), state_spec],
        out_shape=[jax.ShapeDtypeStruct((nb, t_len, C_WIDTH), BF16),
                   jax.ShapeDtypeStruct((nb, n_pair, pw, pw), F32)],
        scratch_shapes=[pltpu.VMEM((nseq, tb + SUBLANES, C_COLS_PAD), F32),
                        pltpu.VMEM((nseq, n_pair, pw, pw), F32)],
        compiler_params=_params("arbitrary", "arbitrary"),
        name="rwkv7",
    )(cols_c.reshape(nb, t_len, C_COLS_PAD), *args, _pad_lora_cols(shift_state), s_bd)
    s_new = jnp.stack([s_out[:, :, :C_HEADDIM, :C_HEADDIM], s_out[:, :, C_HEADDIM:, C_HEADDIM:]], axis=2)
    return o.reshape(nb * t_len, C_WIDTH), s_new.reshape(nb, C_HEADS, C_HEADDIM, C_HEADDIM)


def _merge_kernel(seqs, x_ref, g_ref, sh_ref, sc_ref, gt_ref, oa_ref, ob_ref, oc_ref,
                  wg_ref, wb_ref, wo_ref, out_ref):
    x = x_ref[...]
    rows = x.shape[0]
    h = _norm_mod(x, g_ref[...], sh_ref[...], sc_ref[...], seqs).astype(BF16)
    merged = jnp.zeros((rows, D_MODEL), F32)
    off = 0
    for i, o_ref in enumerate((oa_ref, ob_ref, oc_ref)):
        width = o_ref.shape[1]
        gate = _sigmoid(_nn(h, wg_ref[:, i * D_MODEL:(i + 1) * D_MODEL]))
        merged = merged + gate * _nn(o_ref[...], wb_ref[off:off + width, :])
        off += width
    mix = _nn(merged.astype(BF16), wo_ref[...])
    out = x.reshape(seqs, rows // seqs, D_MODEL) + gt_ref[...] * mix.reshape(seqs, rows // seqs, D_MODEL)
    out_ref[...] = out.reshape(rows, D_MODEL)


def _merge(x, t_len, g, ada3, o_a, o_b, o_c, wg, wb, wo, tile):
    n = x.shape[0]
    seqs, tps = _tile_plan(t_len, tile)
    full = lambda a: pl.BlockSpec(a.shape, lambda i: (0,) * a.ndim)
    ada_spec = lambda j: pl.BlockSpec((seqs, 1, D_MODEL), lambda i: (i // tps, 0, j))
    row = lambda w: pl.BlockSpec((tile, w), lambda i: (i, 0))
    return pl.pallas_call(
        functools.partial(_merge_kernel, seqs),
        grid=(n // tile,),
        in_specs=[row(D_MODEL), full(g), ada_spec(0), ada_spec(1), ada_spec(2),
                  row(A_WIDTH), row(B_INNER), row(C_WIDTH), full(wg), full(wb), full(wo)],
        out_specs=row(D_MODEL),
        out_shape=jax.ShapeDtypeStruct((n, D_MODEL), F32),
        compiler_params=_params("arbitrary"),
        name="merge",
    )(x, g, ada3, ada3, ada3, o_a, o_b, o_c, wg, wb, wo)


def _route(scores, bias):
    lane = _iota2(scores.shape, 1)
    sel = scores + bias

    def partner(x, d, span):
        wrap = (lane & (span - 1)) + d >= span
        fwd = pltpu.roll(x, x.shape[1] - d, 1)
        back = pltpu.roll(x, span - d, 1)
        return jnp.where(wrap, back, fwd), wrap

    rank = jnp.zeros(scores.shape, jnp.int32)
    for d in range(1, EXPERTS_PER_GROUP):
        other, wrap = partner(sel, d, EXPERTS_PER_GROUP)
        beats = (other > sel) | (wrap & (other == sel))
        rank = rank + jnp.where(beats, 1, 0)
    in_top = rank < 2
    g_score = jnp.where(in_top, sel, 0.0)
    top_s = jnp.where(in_top, scores, 0.0)
    g_sum, w_sum = g_score, top_s
    for d in range(1, EXPERTS_PER_GROUP):
        g_sum = g_sum + partner(g_score, d, EXPERTS_PER_GROUP)[0]
        w_sum = w_sum + partner(top_s, d, EXPERTS_PER_GROUP)[0]
    chosen = in_top & (lane < N_EXPERTS)
    for d in range(EXPERTS_PER_GROUP, N_EXPERTS, EXPERTS_PER_GROUP):
        other, wrap = partner(g_sum, d, N_EXPERTS)
        chosen = chosen & ((other < g_sum) | (jnp.logical_not(wrap) & (other == g_sum)))
    return jnp.where(chosen, scores / w_sum, 0.0), chosen


def _group_capacity(tile):
    share = tile // N_EXPERT_GROUPS
    return -(-(share + share // 4) // 64) * 64


def _moe_kernel(seqs, final, cap, x_ref, g_ref, sh_ref, sc_ref, gt_ref, wr_ref, br_ref, fg_ref,
                wg_ref, wu_ref, wd_ref, out_ref, hb_ref, coef_ref, xs_ref, cc_ref, pt_ref, acc_ref, dense_ref):
    e = pl.program_id(1)
    rows = x_ref.shape[0]
    ncomp = N_EXPERT_GROUPS * cap
    lanes = coef_ref.shape[1]

    @pl.when(e == 0)
    def _():
        h = _norm_mod(x_ref[...], g_ref[...], sh_ref[...], sc_ref[...], seqs)
        hb = h.astype(BF16)
        hb_ref[...] = hb
        scores = _sigmoid(_mm(_nn, h, wr_ref[...], 1))
        coef, chosen = _route(scores, br_ref[...])
        coef_ref[...] = coef
        acc_ref[...] = jnp.zeros_like(acc_ref)
        shift = EXPERTS_PER_GROUP.bit_length() - 1
        gmat = jnp.where(((_iota2((lanes, lanes), 0) >> shift) == _iota2((lanes, lanes), 1))
                         & (_iota2((lanes, lanes), 0) < N_EXPERTS), 0.5, 0.0).astype(BF16)
        member = _nn(jnp.where(chosen, 1.0, 0.0).astype(BF16), gmat)
        before = jnp.where(_iota2((rows, rows), 1) < _iota2((rows, rows), 0), 1.0, 0.0).astype(BF16)
        rank = _nn(before, member.astype(BF16))
        count = rank[rows - 1:rows, :] + member[rows - 1:rows, :]
        dense_ref[0] = jnp.max(jnp.where(count > cap, 1, 0))
        lane = _iota2((rows, lanes), 1)
        dest = jnp.sum(member * (lane.astype(F32) * cap + rank), axis=-1, keepdims=True)
        dest_i = dest.astype(jnp.int32)
        dest_row = jnp.broadcast_to(dest, (rows, lanes)).T[0:1, :].astype(jnp.int32)
        perm = jnp.where(_iota2((ncomp, rows), 0) == dest_row, 1.0, 0.0).astype(BF16)
        xs_ref[...] = _nn(perm, hb).astype(BF16)
        c_hi, c_lo = _split2(coef)
        cc_ref[...] = _nn(perm, c_hi) + _nn(perm, c_lo)
        pt_ref[...] = jnp.where(_iota2((rows, ncomp), 1) == dest_i, 1.0, 0.0).astype(BF16)

    def expert(xb, coef):
        c_e = jnp.sum(jnp.where(_iota2(coef.shape, 1) == e, coef, 0.0), axis=-1, keepdims=True)
        hid = _silu(_nn(xb, wg_ref[...])) * _nn(xb, wu_ref[...])
        y = _nn(hid.astype(BF16), wd_ref[...])
        return jnp.where(c_e != 0.0, c_e * y, 0.0)

    @pl.when(dense_ref[0] == 0)
    def _():
        r0 = pl.multiple_of((e // EXPERTS_PER_GROUP) * cap, cap)
        acc_ref[pl.ds(r0, cap), :] += expert(xs_ref[pl.ds(r0, cap), :], cc_ref[pl.ds(r0, cap), :])

    @pl.when(dense_ref[0] != 0)
    def _():
        acc_ref[0:rows, :] += expert(hb_ref[...], coef_ref[...])

    def finish(moe):
        x = x_ref[...]
        out = x.reshape(seqs, rows // seqs, D_MODEL) + gt_ref[...] * moe.reshape(seqs, rows // seqs, D_MODEL)
        out = out.reshape(rows, D_MODEL)
        if final:
            out = out * lax.rsqrt(jnp.mean(out * out, axis=-1, keepdims=True) + RMS_EPS) * fg_ref[...]
        out_ref[...] = out

    last = e == pl.num_programs(1) - 1

    @pl.when(last & (dense_ref[0] == 0))
    def _():
        a_hi, a_lo = _split2(acc_ref[...])
        finish(_nn(pt_ref[...], a_hi) + _nn(pt_ref[...], a_lo))

    @pl.when(last & (dense_ref[0] != 0))
    def _():
        finish(acc_ref[0:rows, :])


def _moe(x, t_len, g, ada3, w_router, b_router, final_g, wg, wu, wd, tile, final):
    n = x.shape[0]
    seqs, tps = _tile_plan(t_len, tile)
    full = lambda a: pl.BlockSpec(a.shape, lambda i, e: (0,) * a.ndim)
    ada_spec = lambda j: pl.BlockSpec((seqs, 1, D_MODEL), lambda i, e: (i // tps, 0, j))
    row = pl.BlockSpec((tile, D_MODEL), lambda i, e: (i, 0))
    expert = lambda a: pl.BlockSpec((None,) + a.shape[1:], lambda i, e: (e, 0, 0))
    cap = _group_capacity(tile)
    ncomp = N_EXPERT_GROUPS * cap
    assert ncomp >= tile
    return pl.pallas_call(
        functools.partial(_moe_kernel, seqs, final, cap),
        grid=(n // tile, N_EXPERTS),
        in_specs=[row, full(g), ada_spec(3), ada_spec(4), ada_spec(5), full(w_router), full(b_router),
                  full(final_g), expert(wg), expert(wu), expert(wd)],
        out_specs=row,
        out_shape=jax.ShapeDtypeStruct((n, D_MODEL), F32),
        scratch_shapes=[pltpu.VMEM((tile, D_MODEL), BF16),
                        pltpu.VMEM((tile, LANES), F32),
                        pltpu.VMEM((ncomp, D_MODEL), BF16),
                        pltpu.VMEM((ncomp, LANES), F32),
                        pltpu.VMEM((tile, ncomp), BF16),
                        pltpu.VMEM((ncomp, D_MODEL), F32),
                        pltpu.SMEM((1,), jnp.int32)],
        compiler_params=_params("arbitrary", "arbitrary"),
        name="moe",
    )(x, g, ada3, ada3, ada3, w_router, b_router, final_g, wg, wu, wd)


def _prep_weights(w_in):
    o_b = A_COLS
    o_c = A_COLS + B_COLS
    o_g = o_c + C_COLS
    wa = w_in[..., :o_b]
    wb = jnp.pad(w_in[..., o_b:o_c], ((0, 0), (0, 0), (0, B_DT_PAD - B_HEADS)))
    wc = _pad_lora_cols(w_in[..., o_c:o_g])
    wgate = w_in[..., o_g:]
    return tuple(w.astype(BF16) for w in (wa, wb, wc, wgate))


def _trunk(x, ada, states, weights, tiles):
    nb, t_len, _ = x.shape
    st_hgrn, st_ssm, st_conv, st_wkv, st_shift = states
    (wa, wb, wc, wgate, w_branch, w_out, w_gate_e, w_up_e, w_down_e, w_router, b_router, p) = weights
    tile_tok, tile_moe = tiles
    depth = wa.shape[0]
    xf = x.reshape(nb * t_len, D_MODEL)
    row = lambda v: v.reshape(1, -1)
    outs = []
    for l in range(depth):
        ada3 = ada[l].reshape(nb, 1, -1)
        g1 = row(p['norm1_g'][l])
        cols_a, cols_b, cols_c = _inproj(xf, t_len, g1, ada3, wa[l], wb[l], wc[l], tile_tok)
        o_a, hgrn_new = _hgrn(cols_a, nb, t_len, l, p['hgrn_lb'], row(p['hgrn_norm_g'][l]), st_hgrn[l])
        o_b, ssm_new = _ssd(cols_b, nb, t_len, p['ssm_conv_w'][l], p['ssm_conv_b'][l], p['ssm_dt_bias'][l],
                            p['ssm_a_log'][l], p['ssm_d'][l], p['ssm_norm_g'][l], st_conv[l], st_ssm[l])
        o_c, wkv_new = _rwkv(cols_c, nb, t_len, st_shift[l], st_wkv[l], p['rwkv_mu'][l], p['rwkv_w0'][l],
                             p['rwkv_w_w2'][l], p['rwkv_a0'][l], p['rwkv_a_w2'][l], p['rwkv_g_w2'][l],
                             p['rwkv_k_k'][l], p['rwkv_k_a'][l], p['rwkv_r_k'][l].reshape(-1),
                             p['rwkv_ln_w'][l], p['rwkv_ln_b'][l])
        assert t_len >= B_CONV - 1
        conv_new = cols_b.reshape(nb, t_len, B_COLS_PAD)[:, t_len - (B_CONV - 1):, B_INNER:B_INNER + B_CONV_DIM]
        shift_new = _unpad_lora_cols(cols_c.reshape(nb, t_len, C_COLS_PAD)[:, t_len - 1:])
        xf = _merge(xf, t_len, g1, ada3, o_a, o_b, o_c, wgate[l], w_branch[l], w_out[l], tile_tok)
        xf = _moe(xf, t_len, row(p['norm2_g'][l]), ada3, w_router, b_router, row(p['final_g']),
                  w_gate_e[l], w_up_e[l], w_down_e[l], tile_moe, final=(l == depth - 1))
        outs.append((hgrn_new, ssm_new, conv_new, wkv_new, shift_new))
    new_states = [jnp.stack(s, axis=0) for s in zip(*outs)]
    return xf.reshape(nb, t_len, D_MODEL), new_states


def kernel(x_prompt, x_sample, c_prompt, c_sample, state_hgrn, state_ssm, state_conv, state_wkv, state_shift,
           w_ada, b_ada, norm1_g, norm2_g, final_g, w_in, hgrn_lb, hgrn_norm_g,
           ssm_conv_w, ssm_conv_b, ssm_dt_bias, ssm_a_log, ssm_d, ssm_norm_g,
           rwkv_mu, rwkv_w0, rwkv_w_w2, rwkv_a0, rwkv_a_w2, rwkv_g_w2, rwkv_k_k, rwkv_k_a, rwkv_r_k,
           rwkv_ln_w, rwkv_ln_b, w_branch, w_out, w_router, b_router, w_gate_e, w_up_e, w_down_e):
    p = {'norm1_g': norm1_g, 'norm2_g': norm2_g, 'final_g': final_g, 'hgrn_lb': hgrn_lb,
         'hgrn_norm_g': hgrn_norm_g, 'ssm_conv_w': ssm_conv_w, 'ssm_conv_b': ssm_conv_b,
         'ssm_dt_bias': ssm_dt_bias, 'ssm_a_log': ssm_a_log, 'ssm_d': ssm_d, 'ssm_norm_g': ssm_norm_g,
         'rwkv_mu': rwkv_mu, 'rwkv_w0': rwkv_w0, 'rwkv_w_w2': rwkv_w_w2, 'rwkv_a0': rwkv_a0,
         'rwkv_a_w2': rwkv_a_w2, 'rwkv_g_w2': rwkv_g_w2, 'rwkv_k_k': rwkv_k_k, 'rwkv_k_a': rwkv_k_a,
         'rwkv_r_k': rwkv_r_k, 'rwkv_ln_w': rwkv_ln_w, 'rwkv_ln_b': rwkv_ln_b}
    depth = w_in.shape[0]
    bp, tp = x_prompt.shape[0], x_prompt.shape[1]
    bs, ts = x_sample.shape[0], x_sample.shape[1]
    wa, wb, wc, wgate = _prep_weights(w_in)
    w_router_pad = jnp.pad(w_router, ((0, 0), (0, LANES - N_EXPERTS)))
    b_router_pad = jnp.pad(b_router.reshape(1, -1), ((0, 0), (0, LANES - N_EXPERTS)))
    weights = (wa, wb, wc, wgate, w_branch.astype(BF16), w_out.astype(BF16),
               w_gate_e.astype(BF16), w_up_e.astype(BF16), w_down_e.astype(BF16),
               w_router_pad, b_router_pad, p)
    ada = _ada(jnp.concatenate([c_prompt, c_sample], axis=0), w_ada, b_ada)

    zeros_p = (jnp.zeros((depth, bp, A_HEADS, A_DK, A_DV), F32),
               jnp.zeros((depth, bp, B_HEADS, B_HEADDIM, B_STATE), F32),
               jnp.zeros((depth, bp, B_CONV - 1, B_CONV_DIM), x_prompt.dtype),
               jnp.zeros((depth, bp, C_HEADS, C_HEADDIM, C_HEADDIM), F32),
               jnp.zeros((depth, bp, 1, C_COLS), x_prompt.dtype))
    tile_p = min(256, tp)
    y_prompt, st_p = _trunk(x_prompt, ada[:, :bp], zeros_p, weights, (tile_p, min(1024, tp)))
    tile_s = min(256, bs * ts)
    y_sample, st_s = _trunk(x_sample, ada[:, bp:], (state_hgrn, state_ssm, state_conv, state_wkv, state_shift),
                            weights, (tile_s, min(512, bs * ts)))
    return (y_prompt, y_sample, *st_p, *st_s)
```

```python
import functools

import jax
import jax.numpy as jnp
from jax import lax
from jax.experimental import pallas as pl
from jax.experimental.pallas import tpu as pltpu

F32 = jnp.float32
BF16 = jnp.bfloat16

D_MODEL = 1024
RMS_EPS = 1e-6
A_HEADS = 4
A_DK = 128
A_DV = 128
A_WIDTH = A_HEADS * A_DV
A_COLS = 4 * A_WIDTH
LB_FLOOR = 1e-30
B_HEADS = 8
B_HEADDIM = 64
B_INNER = B_HEADS * B_HEADDIM
B_GROUPS = 2
B_HPG = B_HEADS // B_GROUPS
B_STATE = 128
B_CONV = 4
B_CONV_DIM = B_INNER + 2 * B_GROUPS * B_STATE
B_COLS = B_INNER + B_CONV_DIM + B_HEADS
C_HEADS = 8
C_HEADDIM = 64
C_WIDTH = C_HEADS * C_HEADDIM
C_DECAY_LORA = 64
C_ICLR_LORA = 64
C_GATE_LORA = 128
C_COLS = 3 * C_WIDTH + C_DECAY_LORA + C_ICLR_LORA + C_GATE_LORA
C_LN_EPS = 64e-5
N_BRANCH = 3
GATE_COLS = N_BRANCH * D_MODEL
MIX_WIDTH = A_WIDTH + B_INNER + C_WIDTH
N_EXPERTS = 16
N_EXPERT_GROUPS = 4
EXPERTS_PER_GROUP = N_EXPERTS // N_EXPERT_GROUPS
D_FF_EXPERT = 512

LANES = 128
SUBLANES = 8
VMEM_LIMIT = 56 * 1024 * 1024

B_DT_PAD = LANES
B_COLS_PAD = B_INNER + B_CONV_DIM + B_DT_PAD
C_LORA_PAD = LANES
C_COLS_PAD = 3 * C_WIDTH + 3 * C_LORA_PAD

TIME_BLOCK = 64


def _nn(a, b):
    return lax.dot_general(a, b, (((1,), (0,)), ((), ())), preferred_element_type=F32)


def _nt(a, b):
    return lax.dot_general(a, b, (((1,), (1,)), ((), ())), preferred_element_type=F32)


def _tn(a, b):
    return lax.dot_general(a, b, (((0,), (0,)), ((), ())), preferred_element_type=F32)


def _split2(x):
    hi = x.astype(BF16)
    lo = (x - hi.astype(F32)).astype(BF16)
    return hi, lo


def _split3(x):
    hi = x.astype(BF16)
    r1 = x - hi.astype(F32)
    mid = r1.astype(BF16)
    lo = (r1 - mid.astype(F32)).astype(BF16)
    return hi, mid, lo


def _mm(dot, a, b, passes):
    if passes == 1:
        return dot(a.astype(BF16), b.astype(BF16))
    a_hi, a_lo = _split2(a)
    b_hi, b_lo = _split2(b)
    return dot(a_hi, b_hi) + (dot(a_hi, b_lo) + dot(a_lo, b_hi))


def _exact_lhs_nn(m_bf16, x):
    hi, mid, lo = _split3(x)
    return _nn(m_bf16, hi) + (_nn(m_bf16, mid) + _nn(m_bf16, lo))


def _exact_rhs_nn(x, m_bf16):
    hi, mid, lo = _split3(x)
    return _nn(hi, m_bf16) + (_nn(mid, m_bf16) + _nn(lo, m_bf16))


def _sigmoid(x):
    return 1.0 / (1.0 + jnp.exp(-x))


def _silu(x):
    return x * _sigmoid(x)


def _softplus(x):
    return jnp.maximum(x, 0.0) + jnp.log1p(jnp.exp(-jnp.abs(x)))


def _iota2(shape, dim):
    return lax.broadcasted_iota(jnp.int32, shape, dim)


def _tril_bf16(n):
    return jnp.where(_iota2((n, n), 1) <= _iota2((n, n), 0), 1.0, 0.0).astype(BF16)


def _norm_mod(x, g, shift, scale, seqs):
    rows, d = x.shape
    y = x * lax.rsqrt(jnp.mean(x * x, axis=-1, keepdims=True) + RMS_EPS) * g
    y = y.reshape(seqs, rows // seqs, d) * (1.0 + scale) + shift
    return y.reshape(rows, d)


def _tile_plan(t_len, tile):
    if tile >= t_len:
        assert tile % t_len == 0
        return tile // t_len, 1
    assert t_len % tile == 0
    return 1, t_len // tile


def _seqs_per_step(nb):
    return 2 if nb % 2 == 0 else 1


def _params(*sem):
    return pltpu.CompilerParams(dimension_semantics=sem, vmem_limit_bytes=VMEM_LIMIT)


def _ada_kernel(c_ref, w_ref, b_ref, o_ref):
    s = _silu(c_ref[...])
    o_ref[...] = _mm(_nn, s, w_ref[...], 1) + b_ref[...]


def _ada(c, w_ada, b_ada):
    nb = c.shape[0]
    depth = w_ada.shape[0]
    n_blk = w_ada.shape[2] // D_MODEL
    return pl.pallas_call(
        _ada_kernel,
        grid=(depth, n_blk),
        in_specs=[pl.BlockSpec((nb, D_MODEL), lambda l, j: (0, 0)),
                  pl.BlockSpec((None, D_MODEL, D_MODEL), lambda l, j: (l, 0, j)),
                  pl.BlockSpec((None, 1, D_MODEL), lambda l, j: (l, 0, j))],
        out_specs=pl.BlockSpec((None, nb, D_MODEL), lambda l, j: (l, 0, j)),
        out_shape=jax.ShapeDtypeStruct((depth, nb, w_ada.shape[2]), F32),
        compiler_params=_params("arbitrary", "arbitrary"),
        name="ada",
    )(c, w_ada, b_ada.reshape(depth, 1, -1))


def _inproj_kernel(seqs, x_ref, g_ref, sh_ref, sc_ref, wa_ref, wb_ref, wc_ref, oa_ref, ob_ref, oc_ref):
    h = _norm_mod(x_ref[...], g_ref[...], sh_ref[...], sc_ref[...], seqs).astype(BF16)
    oa_ref[...] = _nn(h, wa_ref[...])
    ob_ref[...] = _nn(h, wb_ref[...])
    oc_ref[...] = _nn(h, wc_ref[...])


def _inproj(x, t_len, g, ada3, wa, wb, wc, tile):
    n = x.shape[0]
    seqs, tps = _tile_plan(t_len, tile)
    full = lambda a: pl.BlockSpec(a.shape, lambda i: (0,) * a.ndim)
    ada_spec = lambda j: pl.BlockSpec((seqs, 1, D_MODEL), lambda i: (i // tps, 0, j))
    row = lambda w: pl.BlockSpec((tile, w), lambda i: (i, 0))
    return pl.pallas_call(
        functools.partial(_inproj_kernel, seqs),
        grid=(n // tile,),
        in_specs=[row(D_MODEL), full(g), ada_spec(0), ada_spec(1), full(wa), full(wb), full(wc)],
        out_specs=[row(A_COLS), row(B_COLS_PAD), row(C_COLS_PAD)],
        out_shape=[jax.ShapeDtypeStruct((n, A_COLS), F32),
                   jax.ShapeDtypeStruct((n, B_COLS_PAD), F32),
                   jax.ShapeDtypeStruct((n, C_COLS_PAD), F32)],
        compiler_params=_params("arbitrary"),
        name="inproj",
    )(x, g, ada3, ada3, wa, wb, wc)


def _hgrn_kernel(layer, cols_ref, lb_ref, ng_ref, s0_ref, o_ref, sout_ref, st_ref, attn_ref):
    t = pl.program_id(1)
    nseq, tb = cols_ref.shape[0], cols_ref.shape[1]
    unit = [(s, h) for s in range(nseq) for h in range(A_HEADS)]

    @pl.when(t == 0)
    def _():
        for s, h in unit:
            st_ref[s, h] = s0_ref[s, h].T

    rows = [lb_ref[i:i + 1, :] for i in range(lb_ref.shape[0])]
    mx = functools.reduce(jnp.maximum, rows)
    es = [jnp.exp(r - mx) for r in rows]
    den = functools.reduce(lambda a, b: a + b, es)
    lower = jnp.zeros_like(mx)
    for i in range(1, layer + 1):
        lower = lower + es[i] / den

    log_lower = jnp.log(jnp.maximum(lower, LB_FLOOR))
    log1m_lower = jnp.log1p(-lower)
    tril = _tril_bf16(tb)

    def prep(s):
        f_pre = cols_ref[s, :, A_WIDTH:2 * A_WIDTH]
        log_sig = jnp.minimum(f_pre, 0.0) - jnp.log1p(jnp.exp(-jnp.abs(f_pre)))
        bterm = log1m_lower + log_sig
        log_f = jnp.maximum(log_lower, bterm) + jnp.log1p(jnp.exp(-jnp.abs(log_lower - bterm)))
        return dict(q=cols_ref[s, :, 0:A_WIDTH], k=(1.0 - lower) / (1.0 + jnp.exp(f_pre)),
                    v=cols_ref[s, :, 2 * A_WIDTH:3 * A_WIDTH].astype(BF16),
                    b=_exact_lhs_nn(tril, log_f))

    seq = [prep(s) for s in range(nseq)]
    units = range(len(unit))
    col = lambda name: [seq[s][name][:, h * A_DK:(h + 1) * A_DK] for s, h in unit]
    q, kf, vh, bh = col('q'), col('k'), col('v'), col('b')
    st = [st_ref[s, h] for s, h in unit]

    blk = SUBLANES
    nblk = tb // blk
    rowblk = _iota2((tb, 1), 0) >> (blk.bit_length() - 1)
    attn_ref[...] = jnp.zeros_like(attn_ref)
    att_off = []
    for u in units:
        k2 = jnp.concatenate([kf[u][j * blk:(j + 1) * blk]
                              * jnp.exp(bh[u][(j + 1) * blk - 1:(j + 1) * blk] - bh[u][j * blk:(j + 1) * blk])
                              for j in range(nblk)], axis=0)
        acc = None
        for j in range(nblk - 1):
            r0 = (j + 1) * blk
            qe = q[u][r0:] * jnp.exp(bh[u][r0:] - bh[u][r0 - 1:r0])
            part = _mm(_nt, qe, jnp.where(rowblk == j, k2, 0.0), 3)
            part = jnp.concatenate([jnp.zeros((r0, tb), F32), part], axis=0)
            acc = part if acc is None else acc + part
        att_off.append(acc)
        s_, h_ = unit[u]
        for sc in range(tb):
            j0 = (sc // blk) * blk
            live = _iota2((blk, 1), 0) >= sc - j0
            diff = bh[u][j0:j0 + blk] - bh[u][sc:sc + 1]
            dec = jnp.where(live, jnp.exp(jnp.where(live, diff, 0.0)), 0.0)
            attn_ref[s_, h_, j0:j0 + blk, sc:sc + 1] = jnp.sum(
                q[u][j0:j0 + blk] * kf[u][sc:sc + 1] * dec, axis=-1, keepdims=True)

    g_all = [cols_ref[s, :, 3 * A_WIDTH:4 * A_WIDTH] for s in range(nseq)]
    for u in units:
        s_, h_ = unit[u]
        hs = slice(h_ * A_DV, (h_ + 1) * A_DV)
        att = attn_ref[s_, h_] + att_off[u]
        oh = (_nt((q[u] * jnp.exp(bh[u])).astype(BF16), st[u].astype(BF16))
              + _nn(att.astype(BF16), vh[u]))
        b_last = bh[u][tb - 1:tb]
        kdec = kf[u] * jnp.exp(b_last - bh[u])
        st_ref[s_, h_] = st[u] * jnp.exp(b_last) + _tn(vh[u], kdec.astype(BF16))
        on = oh * lax.rsqrt(jnp.mean(oh * oh, axis=-1, keepdims=True) + RMS_EPS) * ng_ref[:, hs]
        o_ref[s_, :, hs] = (on * _silu(g_all[s_][:, hs])).astype(o_ref.dtype)

    @pl.when(t == pl.num_programs(1) - 1)
    def _():
        for s, h in unit:
            sout_ref[s, h] = st_ref[s, h].T


def _hgrn(cols_a, nb, t_len, layer, hgrn_lb, norm_g, state0):
    tb = min(TIME_BLOCK, t_len)
    nt = t_len // tb
    nseq = _seqs_per_step(nb)
    state_spec = pl.BlockSpec((nseq, A_HEADS, A_DK, A_DV), lambda b, t: (b, 0, 0, 0))
    o, s_out = pl.pallas_call(
        functools.partial(_hgrn_kernel, layer),
        grid=(nb // nseq, nt),
        in_specs=[pl.BlockSpec((nseq, tb, A_COLS), lambda b, t: (b, t, 0)),
                  pl.BlockSpec(hgrn_lb.shape, lambda b, t: (0, 0)),
                  pl.BlockSpec((1, A_WIDTH), lambda b, t: (0, 0)),
                  state_spec],
        out_specs=[pl.BlockSpec((nseq, tb, A_WIDTH), lambda b, t: (b, t, 0)), state_spec],
        out_shape=[jax.ShapeDtypeStruct((nb, t_len, A_WIDTH), BF16),
                   jax.ShapeDtypeStruct((nb, A_HEADS, A_DK, A_DV), F32)],
        scratch_shapes=[pltpu.VMEM((nseq, A_HEADS, A_DV, A_DK), F32),
                        pltpu.VMEM((nseq, A_HEADS, tb, tb), F32)],
        compiler_params=_params("arbitrary", "arbitrary"),
        name="hgrn2",
    )(cols_a.reshape(nb, t_len, A_COLS), hgrn_lb, norm_g, state0)
    return o.reshape(nb * t_len, A_WIDTH), s_out


def _ssd_kernel(cols_ref, cw_ref, cb_ref, dtb_ref, alog_ref, dsk_ref, ng_ref, cs0_ref, s0_ref,
                o_ref, sout_ref, ext_ref, st_ref):
    t = pl.program_id(1)
    nseq, tb = cols_ref.shape[0], cols_ref.shape[1]
    n_pair = B_HEADS // 2
    pair_w = 2 * B_HEADDIM

    @pl.when(t == 0)
    def _():
        ext_ref[:, 0:SUBLANES, :] = jnp.zeros((nseq, SUBLANES, B_CONV_DIM), F32)
        ext_ref[:, SUBLANES - (B_CONV - 1):SUBLANES, :] = cs0_ref[...]
        st_ref[...] = s0_ref[...]

    rnd = lambda a: a.astype(BF16).astype(F32)
    tril = _tril_bf16(tb)
    causal = _iota2((tb, tb), 1) <= _iota2((tb, tb), 0)
    lane_lo = _iota2((1, pair_w), 1) < B_HEADDIM
    row_lo = _iota2((pair_w, 1), 0) < B_HEADDIM
    neg_a = -jnp.exp(alog_ref[...])
    states = [[st_ref[s, p] for p in range(n_pair)] for s in range(nseq)]
    new_states = []

    for s in range(nseq):
        z = cols_ref[s, :, 0:B_INNER]
        ext_ref[s, SUBLANES:SUBLANES + tb, :] = cols_ref[s, :, B_INNER:B_INNER + B_CONV_DIM]
        conv = None
        for j in range(B_CONV):
            tap = rnd(cw_ref[j:j + 1, :]) * rnd(ext_ref[s, pl.ds(SUBLANES - (B_CONV - 1) + j, tb), :])
            conv = tap if conv is None else conv + tap
        conv = conv + cb_ref[...]
        ext_ref[s, 0:SUBLANES, :] = ext_ref[s, tb:tb + SUBLANES, :]
        xbc = _silu(conv)
        xs = xbc[:, 0:B_INNER]
        bm = xbc[:, B_INNER:B_INNER + B_GROUPS * B_STATE].astype(BF16)
        cm = xbc[:, B_INNER + B_GROUPS * B_STATE:].astype(BF16)

        dt = _softplus(cols_ref[s, :, B_INNER + B_CONV_DIM:] + dtb_ref[...])
        a = _exact_lhs_nn(tril, dt * neg_a)
        a_t = a.T
        dt_t = dt.T
        cb = [_nt(cm[:, g * B_STATE:(g + 1) * B_STATE], bm[:, g * B_STATE:(g + 1) * B_STATE])
              for g in range(B_GROUPS)]

        y_pairs = []
        for p in range(n_pair):
            grp = (2 * p) // B_HPG
            bg = bm[:, grp * B_STATE:(grp + 1) * B_STATE]
            cg = cm[:, grp * B_STATE:(grp + 1) * B_STATE]
            xp = xs[:, p * pair_w:(p + 1) * pair_w]
            sp = states[s][p]
            y_p = _nt(cg, sp.astype(BF16))
            a_cols, w_cols, e_last = [], [], []
            for r in range(2):
                hh = 2 * p + r
                a_col = a[:, hh:hh + 1]
                a_last = a[tb - 1:tb, hh:hh + 1]
                a_cols.append(a_col)
                w_cols.append(jnp.exp(a_last - a_col) * dt[:, hh:hh + 1])
                e_last.append(jnp.exp(a_last))
            y_p = y_p * jnp.where(lane_lo, jnp.exp(a_cols[0]), jnp.exp(a_cols[1]))
            for r in range(2):
                hh = 2 * p + r
                seg = jnp.where(causal, jnp.exp(jnp.where(causal, a_cols[r] - a_t[hh:hh + 1, :], 0.0)), 0.0)
                m = cb[grp] * seg * dt_t[hh:hh + 1, :]
                half = lane_lo if r == 0 else jnp.logical_not(lane_lo)
                y_p = y_p + _nn(m.astype(BF16), jnp.where(half, xp, 0.0).astype(BF16))
            xw = xp * jnp.where(lane_lo, w_cols[0], w_cols[1])
            new_states.append(sp * jnp.where(row_lo, e_last[0], e_last[1]) + _tn(xw.astype(BF16), bg))
            y_pairs.append(y_p)

        y = jnp.concatenate(y_pairs, axis=-1) + dsk_ref[...] * xs
        yz = y * _silu(z)
        o_ref[s] = (yz * lax.rsqrt(jnp.mean(yz * yz, axis=-1, keepdims=True) + RMS_EPS)
                    * ng_ref[...]).astype(o_ref.dtype)

    for s in range(nseq):
        for p in range(n_pair):
            st_ref[s, p] = new_states[s * n_pair + p]

    @pl.when(t == pl.num_programs(1) - 1)
    def _():
        sout_ref[...] = st_ref[...]


def _ssd(cols_b, nb, t_len, conv_w, conv_b, dt_bias, a_log, d_skip, norm_g, conv_state, ssm_state):
    tb = min(TIME_BLOCK, t_len)
    nt = t_len // tb
    n_pair = B_HEADS // 2
    pair_w = 2 * B_HEADDIM
    pad8 = lambda v: jnp.pad(v.reshape(1, B_HEADS), ((0, 0), (0, B_DT_PAD - B_HEADS)))
    const = lambda a: pl.BlockSpec(a.shape, lambda b, t: (0,) * a.ndim)
    args = (conv_w, conv_b.reshape(1, -1), pad8(dt_bias), pad8(a_log),
            jnp.repeat(d_skip, B_HEADDIM).reshape(1, B_INNER), norm_g.reshape(1, -1))
    nseq = _seqs_per_step(nb)
    state_spec = pl.BlockSpec((nseq, n_pair, pair_w, B_STATE), lambda b, t: (b, 0, 0, 0))
    o, s_out = pl.pallas_call(
        _ssd_kernel,
        grid=(nb // nseq, nt),
        in_specs=[pl.BlockSpec((nseq, tb, B_COLS_PAD), lambda b, t: (b, t, 0))]
                 + [const(a) for a in args]
                 + [pl.BlockSpec((nseq, B_CONV - 1, B_CONV_DIM), lambda b, t: (b, 0, 0)), state_spec],
        out_specs=[pl.BlockSpec((nseq, tb, B_INNER), lambda b, t: (b, t, 0)), state_spec],
        out_shape=[jax.ShapeDtypeStruct((nb, t_len, B_INNER), BF16),
                   jax.ShapeDtypeStruct((nb, n_pair, pair_w, B_STATE), F32)],
        scratch_shapes=[pltpu.VMEM((nseq, tb + SUBLANES, B_CONV_DIM), F32),
                        pltpu.VMEM((nseq, n_pair, pair_w, B_STATE), F32)],
        compiler_params=_params("arbitrary", "arbitrary"),
        name="ssd",
    )(cols_b.reshape(nb, t_len, B_COLS_PAD), *args, conv_state, ssm_state.reshape(nb, n_pair, pair_w, B_STATE))
    return o.reshape(nb * t_len, B_INNER), s_out.reshape(nb, B_HEADS, B_HEADDIM, B_STATE)


RWKV_PASSES = 3


def _rwkv_kernel(cols_ref, mu_ref, w0_ref, ww2_ref, a0_ref, aw2_ref, gw2_ref, kk_ref, ka_ref, rk_ref,
                 lnw_ref, lnb_ref, sh0_ref, s0_ref, o_ref, sout_ref, ext_ref, st_ref):
    t = pl.program_id(1)
    nseq, tb = cols_ref.shape[0], cols_ref.shape[1]
    n_pair = C_HEADS // 2
    pw = 2 * C_HEADDIM
    s3 = 3 * C_WIDTH
    mm = functools.partial(_mm, passes=RWKV_PASSES)

    @pl.when(t == 0)
    def _():
        ext_ref[:, 0:SUBLANES, :] = jnp.zeros((nseq, SUBLANES, C_COLS_PAD), F32)
        ext_ref[:, SUBLANES - 1:SUBLANES, :] = sh0_ref[...]
        st_ref[...] = s0_ref[...]

    head_of = lambda n, d: _iota2((n, n), d) >> (C_HEADDIM.bit_length() - 1)
    seg_ones = jnp.where(head_of(C_WIDTH, 0) == head_of(C_WIDTH, 1), 1.0, 0.0).astype(BF16)

    def segsum(x):
        hi, lo = _split2(x)
        return _nn(hi, seg_ones) + _nn(lo, seg_ones)

    tril = _tril_bf16(tb)
    rnd = lambda a: a.astype(BF16).astype(F32)

    def prep(s):
        cur = cols_ref[s]
        ext_ref[s, SUBLANES:SUBLANES + tb, :] = cur
        prev = ext_ref[s, pl.ds(SUBLANES - 1, tb), :]
        ext_ref[s, 0:SUBLANES, :] = ext_ref[s, tb:tb + SUBLANES, :]
        mixed = cur + (prev - cur) * mu_ref[...]
        r = mixed[:, 0:C_WIDTH]
        k = mixed[:, C_WIDTH:2 * C_WIDTH]
        v = mixed[:, 2 * C_WIDTH:s3]
        w_lo = mixed[:, s3:s3 + C_LORA_PAD]
        a_lo = mixed[:, s3 + C_LORA_PAD:s3 + 2 * C_LORA_PAD]
        g_lo = mixed[:, s3 + 2 * C_LORA_PAD:]
        w_raw = -_softplus(-(w0_ref[...] + _mm(_nn, jnp.tanh(w_lo), ww2_ref[...], 1))) - 0.5
        logw = -jnp.exp(w_raw)
        a = _sigmoid(a0_ref[...] + _mm(_nn, a_lo, aw2_ref[...], 1))
        g = _mm(_nn, _sigmoid(g_lo), gw2_ref[...], 1)
        kk = k * kk_ref[...]
        kk = kk / jnp.maximum(jnp.sqrt(segsum(kk * kk)), 1e-12)
        k2 = k * (1.0 + (a - 1.0) * ka_ref[...])
        beta = kk * a
        c = _exact_lhs_nn(tril, logw)
        c_last = c[tb - 1:tb, :]
        g_inv = jnp.exp(-c)
        g_hat = jnp.exp(c_last - c)
        return dict(r=r, v=v, g=g, k2=k2, gam_last=jnp.exp(c_last),
                    ab=-rnd(kk) * jnp.exp(c - logw), rb=rnd(r) * jnp.exp(c),
                    bt=beta * g_inv, kt=k2 * g_inv, bh=beta * g_hat, kh=k2 * g_hat)

    seq = [prep(s) for s in range(nseq)]

    n2 = 2 * tb
    ri = _iota2((n2, n2), 0)
    ci = _iota2((n2, n2), 1)
    keep = jnp.where(ri < tb, ci & (tb - 1), (ci & (tb - 1)) - 1) < (ri & (tb - 1))
    eye = jnp.where(ri == ci, 1.0, 0.0)
    left_h = _iota2((tb, n2), 1) < tb
    lane0 = _iota2((1, pw), 1) < C_HEADDIM
    rows_top = _iota2((n2, 1), 0) < tb
    d_own = (rows_top & lane0) | (jnp.logical_not(rows_top) & jnp.logical_not(lane0))
    blockdiag = (_iota2((pw, pw), 0) < C_HEADDIM) == (_iota2((pw, pw), 1) < C_HEADDIM)
    stack = lambda a, b: jnp.concatenate([a, b], axis=0)
    unit = [(s, p) for s in range(nseq) for p in range(n_pair)]
    pairs = range(len(unit))
    col = lambda name: [seq[s][name][:, p * pw:(p + 1) * pw] for s, p in unit]
    sp = [st_ref[s, p] for s, p in unit]
    vp, ab, rb, bt, kt = col('v'), col('ab'), col('rb'), col('bt'), col('kt')
    x = [stack(ab[p], rb[p]) for p in pairs]
    yk = [jnp.concatenate([jnp.where(lane0, bt[p], 0.0), jnp.where(lane0, kt[p], 0.0),
                           jnp.where(lane0, 0.0, kt[p]), jnp.where(lane0, 0.0, bt[p])], axis=0)
          for p in pairs]
    pp = [mm(_nt, x[p], yk[p]) for p in pairs]
    blk0 = [jnp.where(keep, pp[p][:, 0:n2], 0.0) for p in pairs]
    blk1 = [jnp.where(keep, pp[p][:, n2:2 * n2], 0.0) for p in pairs]
    a_bd = [stack(jnp.where(left_h, blk0[p][0:tb], 0.0), jnp.where(left_h, 0.0, blk1[p][0:tb])) for p in pairs]
    q_ak = [stack(jnp.where(left_h, 0.0, blk0[p][0:tb]), jnp.where(left_h, blk1[p][0:tb], 0.0)) for p in pairs]
    r_mat = [stack(blk1[p][tb:n2], blk0[p][tb:n2]) for p in pairs]
    gs = [mm(_nt, x[p], sp[p]) for p in pairs]
    w = [jnp.where(d_own, stack(gs[p][0:tb], gs[p][0:tb]) + mm(_nn, q_ak[p], stack(vp[p], vp[p])), 0.0)
         for p in pairs]
    t_inv = [eye + a_bd[p] for p in pairs]
    a_pow = list(a_bd)
    for _ in range(max(1, (tb - 1).bit_length()) - 1):
        a_pow = [_mm(_nn, a_pow[p], a_pow[p], 1) for p in pairs]
        t_inv = [t_inv[p] + _mm(_nn, t_inv[p], a_pow[p], 1) for p in pairs]
    u = [_mm(_nn, t_inv[p], w[p], 1) for p in pairs]
    res = [w[p] - u[p] + mm(_nn, a_bd[p], u[p]) for p in pairs]
    u = [u[p] + _mm(_nn, t_inv[p], res[p], 1) for p in pairs]
    zz = [u[p] + stack(jnp.where(lane0, 0.0, vp[p]), jnp.where(lane0, vp[p], 0.0)) for p in pairs]
    yf = [jnp.where(d_own, 0.0, mm(_nn, r_mat[p], zz[p]) + stack(gs[p][tb:n2], gs[p][tb:n2])) for p in pairs]
    y_pairs = [yf[p][0:tb] + yf[p][tb:n2] for p in pairs]
    uv = [stack(u[p][0:tb] + u[p][tb:n2], vp[p]) for p in pairs]
    bh, kh, gam = col('bh'), col('kh'), col('gam_last')
    s_new = [sp[p] * gam[p] + jnp.where(blockdiag, mm(_tn, uv[p], stack(bh[p], kh[p])), 0.0) for p in pairs]
    for i, (s, p) in enumerate(unit):
        st_ref[s, p] = s_new[i]

    inv_n = 1.0 / C_HEADDIM
    for s in range(nseq):
        d = seq[s]
        y = jnp.concatenate(y_pairs[s * n_pair:(s + 1) * n_pair], axis=-1)
        mean = segsum(y) * inv_n
        dlt = y - mean
        var = segsum(dlt * dlt) * inv_n
        yn = dlt * lax.rsqrt(var + C_LN_EPS) * lnw_ref[...] + lnb_ref[...]
        bonus = segsum(d['r'] * d['k2'] * rk_ref[...]) * d['v']
        o_ref[s] = ((yn + bonus) * d['g']).astype(o_ref.dtype)

    @pl.when(t == pl.num_programs(1) - 1)
    def _():
        sout_ref[...] = st_ref[...]


def _pad_lora_cols(a):
    s3 = 3 * C_WIDTH
    z = jnp.zeros(a.shape[:-1] + (C_LORA_PAD - C_DECAY_LORA,), a.dtype)
    return jnp.concatenate([a[..., :s3 + C_DECAY_LORA], z,
                            a[..., s3 + C_DECAY_LORA:s3 + C_DECAY_LORA + C_ICLR_LORA], z,
                            a[..., s3 + C_DECAY_LORA + C_ICLR_LORA:]], axis=-1)


def _unpad_lora_cols(a):
    s3 = 3 * C_WIDTH
    return jnp.concatenate([a[..., :s3 + C_DECAY_LORA],
                            a[..., s3 + C_LORA_PAD:s3 + C_LORA_PAD + C_ICLR_LORA],
                            a[..., s3 + 2 * C_LORA_PAD:]], axis=-1)


def _rwkv(cols_c, nb, t_len, shift_state, wkv_state, mu, w0, w_w2, a0, a_w2, g_w2, k_k, k_a, r_k, ln_w, ln_b):
    tb = min(TIME_BLOCK, t_len)
    nt = t_len // tb
    n_pair = C_HEADS // 2
    pw = 2 * C_HEADDIM
    row = lambda v: v.reshape(1, -1)
    pad_rows = lambda w: jnp.pad(w, ((0, C_LORA_PAD - w.shape[0]), (0, 0)))
    args = (row(_pad_lora_cols(mu)), row(w0), pad_rows(w_w2), row(a0), pad_rows(a_w2), g_w2,
            row(k_k), row(k_a), row(r_k), row(ln_w), row(ln_b))
    const = lambda a: pl.BlockSpec(a.shape, lambda b, t: (0,) * a.ndim)
    s4 = wkv_state.reshape(nb, n_pair, 2, C_HEADDIM, C_HEADDIM)
    zero = jnp.zeros_like(s4[:, :, 0])
    s_bd = jnp.concatenate([jnp.concatenate([s4[:, :, 0], zero], axis=-1),
                            jnp.concatenate([zero, s4[:, :, 1]], axis=-1)], axis=-2)
    nseq = _seqs_per_step(nb)
    state_spec = pl.BlockSpec((nseq, n_pair, pw, pw), lambda b, t: (b, 0, 0, 0))
    o, s_out = pl.pallas_call(
        _rwkv_kernel,
        grid=(nb // nseq, nt),
        in_specs=[pl.BlockSpec((nseq, tb, C_COLS_PAD), lambda b, t: (b, t, 0))]
                 + [const(a) for a in args]
                 + [pl.BlockSpec((nseq, 1, C_COLS_PAD), lambda b, t: (b, 0, 0)), state_spec],
        out_specs=[pl.BlockSpec((nseq, tb, C_WIDTH), lambda b, t: (b, t, 0)), state_spec],
        out_shape=[jax.ShapeDtypeStruct((nb, t_len, C_WIDTH), BF16),
                   jax.ShapeDtypeStruct((nb, n_pair, pw, pw), F32)],
        scratch_shapes=[pltpu.VMEM((nseq, tb + SUBLANES, C_COLS_PAD), F32),
                        pltpu.VMEM((nseq, n_pair, pw, pw), F32)],
        compiler_params=_params("arbitrary", "arbitrary"),
        name="rwkv7",
    )(cols_c.reshape(nb, t_len, C_COLS_PAD), *args, _pad_lora_cols(shift_state), s_bd)
    s_new = jnp.stack([s_out[:, :, :C_HEADDIM, :C_HEADDIM], s_out[:, :, C_HEADDIM:, C_HEADDIM:]], axis=2)
    return o.reshape(nb * t_len, C_WIDTH), s_new.reshape(nb, C_HEADS, C_HEADDIM, C_HEADDIM)


def _merge_kernel(seqs, x_ref, g_ref, sh_ref, sc_ref, gt_ref, oa_ref, ob_ref, oc_ref,
                  wg_ref, wb_ref, wo_ref, out_ref):
    x = x_ref[...]
    rows = x.shape[0]
    h = _norm_mod(x, g_ref[...], sh_ref[...], sc_ref[...], seqs).astype(BF16)
    merged = jnp.zeros((rows, D_MODEL), F32)
    off = 0
    for i, o_ref in enumerate((oa_ref, ob_ref, oc_ref)):
        width = o_ref.shape[1]
        gate = _sigmoid(_nn(h, wg_ref[:, i * D_MODEL:(i + 1) * D_MODEL]))
        merged = merged + gate * _nn(o_ref[...], wb_ref[off:off + width, :])
        off += width
    mix = _nn(merged.astype(BF16), wo_ref[...])
    out = x.reshape(seqs, rows // seqs, D_MODEL) + gt_ref[...] * mix.reshape(seqs, rows // seqs, D_MODEL)
    out_ref[...] = out.reshape(rows, D_MODEL)


def _merge(x, t_len, g, ada3, o_a, o_b, o_c, wg, wb, wo, tile):
    n = x.shape[0]
    seqs, tps = _tile_plan(t_len, tile)
    full = lambda a: pl.BlockSpec(a.shape, lambda i: (0,) * a.ndim)
    ada_spec = lambda j: pl.BlockSpec((seqs, 1, D_MODEL), lambda i: (i // tps, 0, j))
    row = lambda w: pl.BlockSpec((tile, w), lambda i: (i, 0))
    return pl.pallas_call(
        functools.partial(_merge_kernel, seqs),
        grid=(n // tile,),
        in_specs=[row(D_MODEL), full(g), ada_spec(0), ada_spec(1), ada_spec(2),
                  row(A_WIDTH), row(B_INNER), row(C_WIDTH), full(wg), full(wb), full(wo)],
        out_specs=row(D_MODEL),
        out_shape=jax.ShapeDtypeStruct((n, D_MODEL), F32),
        compiler_params=_params("arbitrary"),
        name="merge",
    )(x, g, ada3, ada3, ada3, o_a, o_b, o_c, wg, wb, wo)


def _route(scores, bias):
    lane = _iota2(scores.shape, 1)
    sel = scores + bias

    def partner(x, d, span):
        wrap = (lane & (span - 1)) + d >= span
        fwd = pltpu.roll(x, x.shape[1] - d, 1)
        back = pltpu.roll(x, span - d, 1)
        return jnp.where(wrap, back, fwd), wrap

    rank = jnp.zeros(scores.shape, jnp.int32)
    for d in range(1, EXPERTS_PER_GROUP):
        other, wrap = partner(sel, d, EXPERTS_PER_GROUP)
        beats = (other > sel) | (wrap & (other == sel))
        rank = rank + jnp.where(beats, 1, 0)
    in_top = rank < 2
    g_score = jnp.where(in_top, sel, 0.0)
    top_s = jnp.where(in_top, scores, 0.0)
    g_sum, w_sum = g_score, top_s
    for d in range(1, EXPERTS_PER_GROUP):
        g_sum = g_sum + partner(g_score, d, EXPERTS_PER_GROUP)[0]
        w_sum = w_sum + partner(top_s, d, EXPERTS_PER_GROUP)[0]
    chosen = in_top & (lane < N_EXPERTS)
    for d in range(EXPERTS_PER_GROUP, N_EXPERTS, EXPERTS_PER_GROUP):
        other, wrap = partner(g_sum, d, N_EXPERTS)
        chosen = chosen & ((other < g_sum) | (jnp.logical_not(wrap) & (other == g_sum)))
    return jnp.where(chosen, scores / w_sum, 0.0), chosen


MOE_ROW_BLOCK = 128


def _expert_out(xb, c_e, wg, wu, wd):
    hid = _silu(_nn(xb, wg)) * _nn(xb, wu)
    y = _nn(hid.astype(BF16), wd)
    return jnp.where(c_e != 0.0, c_e * y, 0.0)


def _lane_pick(a, lane_idx):
    return jnp.sum(jnp.where(_iota2(a.shape, 1) == lane_idx, a, 0.0), axis=-1, keepdims=True)


def _residual_out(seqs, final, x, moe, gate, final_g):
    rows = x.shape[0]
    out = x.reshape(seqs, rows // seqs, D_MODEL) + gate * moe.reshape(seqs, rows // seqs, D_MODEL)
    out = out.reshape(rows, D_MODEL)
    if final:
        out = out * lax.rsqrt(jnp.mean(out * out, axis=-1, keepdims=True) + RMS_EPS) * final_g
    return out


def _moe_kernel(seqs, final, x_ref, g_ref, sh_ref, sc_ref, gt_ref, wr_ref, br_ref, fg_ref,
                wg_ref, wu_ref, wd_ref, out_ref, xs_ref, cc_ref, pt_ref, acc_ref, seg_ref):
    e = pl.program_id(1)
    rows = x_ref.shape[0]
    ncomp = xs_ref.shape[0]
    lanes = cc_ref.shape[1]
    rb = MOE_ROW_BLOCK

    @pl.when(e == 0)
    def _():
        h = _norm_mod(x_ref[...], g_ref[...], sh_ref[...], sc_ref[...], seqs)
        scores = _sigmoid(_mm(_nn, h, wr_ref[...], 1))
        coef, chosen = _route(scores, br_ref[...])
        shift = EXPERTS_PER_GROUP.bit_length() - 1
        li, lj = _iota2((lanes, lanes), 0), _iota2((lanes, lanes), 1)
        gmat = jnp.where(((li >> shift) == lj) & (li < N_EXPERTS), 0.5, 0.0).astype(BF16)
        member = _nn(jnp.where(chosen, 1.0, 0.0).astype(BF16), gmat)
        before = jnp.where(_iota2((rows, rows), 1) < _iota2((rows, rows), 0), 1.0, 0.0).astype(BF16)
        rank = _nn(before, member.astype(BF16))
        count = rank[rows - 1:rows, :] + member[rows - 1:rows, :]
        blocks = jnp.floor((count + (rb - 1)) * (1.0 / rb))
        earlier = jnp.where(li < lj, 1.0, 0.0).astype(BF16)
        start = _nn(jnp.broadcast_to(blocks * rb, (SUBLANES, lanes)).astype(BF16), earlier)[0:1, :]
        lane_row = _iota2((1, lanes), 1)
        for gi in range(N_EXPERT_GROUPS):
            seg_ref[gi] = jnp.sum(jnp.where(lane_row == gi, start, 0.0)).astype(jnp.int32)
            seg_ref[N_EXPERT_GROUPS + gi] = jnp.sum(jnp.where(lane_row == gi, blocks, 0.0)).astype(jnp.int32)
        dest = jnp.sum(member * (start + rank), axis=-1, keepdims=True)
        dest_row = jnp.broadcast_to(dest, (rows, lanes)).T[0:1, :].astype(jnp.int32)
        perm = jnp.where(_iota2((ncomp, rows), 0) == dest_row, 1.0, 0.0).astype(BF16)
        xs_ref[...] = _nn(perm, h.astype(BF16)).astype(BF16)
        c_hi, c_lo = _split2(coef)
        cc_ref[...] = _nn(perm, c_hi) + _nn(perm, c_lo)
        pt_ref[...] = jnp.where(_iota2((rows, ncomp), 1) == dest.astype(jnp.int32), 1.0, 0.0).astype(BF16)
        acc_ref[...] = jnp.zeros_like(acc_ref)

    grp = e // EXPERTS_PER_GROUP

    def block(k, carry):
        r0 = pl.multiple_of(seg_ref[grp] + k * rb, rb)
        acc_ref[pl.ds(r0, rb), :] += _expert_out(xs_ref[pl.ds(r0, rb), :], _lane_pick(cc_ref[pl.ds(r0, rb), :], e),
                                                 wg_ref[...], wu_ref[...], wd_ref[...])
        return carry

    lax.fori_loop(0, seg_ref[N_EXPERT_GROUPS + grp], block, 0)

    @pl.when(e == pl.num_programs(1) - 1)
    def _():
        a_hi, a_lo = _split2(acc_ref[...])
        moe = _nn(pt_ref[...], a_hi) + _nn(pt_ref[...], a_lo)
        out_ref[...] = _residual_out(seqs, final, x_ref[...], moe, gt_ref[...], fg_ref[...])


def _moe(x, t_len, g, ada3, w_router, b_router, final_g, wg, wu, wd, tile, final):
    n = x.shape[0]
    seqs, tps = _tile_plan(t_len, tile)
    ncomp = tile + N_EXPERT_GROUPS * MOE_ROW_BLOCK
    full = lambda a: pl.BlockSpec(a.shape, lambda i, e: (0,) * a.ndim)
    ada_spec = lambda j: pl.BlockSpec((seqs, 1, D_MODEL), lambda i, e: (i // tps, 0, j))
    row = pl.BlockSpec((tile, D_MODEL), lambda i, e: (i, 0))
    expert = lambda a: pl.BlockSpec((None,) + a.shape[1:], lambda i, e: (e, 0, 0))
    return pl.pallas_call(
        functools.partial(_moe_kernel, seqs, final),
        grid=(n // tile, N_EXPERTS),
        in_specs=[row, full(g), ada_spec(3), ada_spec(4), ada_spec(5), full(w_router), full(b_router),
                  full(final_g), expert(wg), expert(wu), expert(wd)],
        out_specs=row,
        out_shape=jax.ShapeDtypeStruct((n, D_MODEL), F32),
        scratch_shapes=[pltpu.VMEM((ncomp, D_MODEL), BF16),
                        pltpu.VMEM((ncomp, LANES), F32),
                        pltpu.VMEM((tile, ncomp), BF16),
                        pltpu.VMEM((ncomp, D_MODEL), F32),
                        pltpu.SMEM((2 * N_EXPERT_GROUPS,), jnp.int32)],
        compiler_params=_params("arbitrary", "arbitrary"),
        name="moe",
    )(x, g, ada3, ada3, ada3, w_router, b_router, final_g, wg, wu, wd)


def _prep_weights(w_in):
    o_b = A_COLS
    o_c = A_COLS + B_COLS
    o_g = o_c + C_COLS
    wa = w_in[..., :o_b]
    wb = jnp.pad(w_in[..., o_b:o_c], ((0, 0), (0, 0), (0, B_DT_PAD - B_HEADS)))
    wc = _pad_lora_cols(w_in[..., o_c:o_g])
    wgate = w_in[..., o_g:]
    return tuple(w.astype(BF16) for w in (wa, wb, wc, wgate))


def _trunk(x, ada, states, weights, tiles):
    nb, t_len, _ = x.shape
    st_hgrn, st_ssm, st_conv, st_wkv, st_shift = states
    (wa, wb, wc, wgate, w_branch, w_out, w_gate_e, w_up_e, w_down_e, w_router, b_router, p) = weights
    tile_tok, tile_moe = tiles
    depth = wa.shape[0]
    xf = x.reshape(nb * t_len, D_MODEL)
    row = lambda v: v.reshape(1, -1)
    outs = []
    for l in range(depth):
        ada3 = ada[l].reshape(nb, 1, -1)
        g1 = row(p['norm1_g'][l])
        cols_a, cols_b, cols_c = _inproj(xf, t_len, g1, ada3, wa[l], wb[l], wc[l], tile_tok)
        o_a, hgrn_new = _hgrn(cols_a, nb, t_len, l, p['hgrn_lb'], row(p['hgrn_norm_g'][l]), st_hgrn[l])
        o_b, ssm_new = _ssd(cols_b, nb, t_len, p['ssm_conv_w'][l], p['ssm_conv_b'][l], p['ssm_dt_bias'][l],
                            p['ssm_a_log'][l], p['ssm_d'][l], p['ssm_norm_g'][l], st_conv[l], st_ssm[l])
        o_c, wkv_new = _rwkv(cols_c, nb, t_len, st_shift[l], st_wkv[l], p['rwkv_mu'][l], p['rwkv_w0'][l],
                             p['rwkv_w_w2'][l], p['rwkv_a0'][l], p['rwkv_a_w2'][l], p['rwkv_g_w2'][l],
                             p['rwkv_k_k'][l], p['rwkv_k_a'][l], p['rwkv_r_k'][l].reshape(-1),
                             p['rwkv_ln_w'][l], p['rwkv_ln_b'][l])
        assert t_len >= B_CONV - 1
        conv_new = cols_b.reshape(nb, t_len, B_COLS_PAD)[:, t_len - (B_CONV - 1):, B_INNER:B_INNER + B_CONV_DIM]
        shift_new = _unpad_lora_cols(cols_c.reshape(nb, t_len, C_COLS_PAD)[:, t_len - 1:])
        xf = _merge(xf, t_len, g1, ada3, o_a, o_b, o_c, wgate[l], w_branch[l], w_out[l], tile_tok)
        xf = _moe(xf, t_len, row(p['norm2_g'][l]), ada3, w_router, b_router, row(p['final_g']),
                  w_gate_e[l], w_up_e[l], w_down_e[l], tile_moe, final=(l == depth - 1))
        outs.append((hgrn_new, ssm_new, conv_new, wkv_new, shift_new))
    new_states = [jnp.stack(s, axis=0) for s in zip(*outs)]
    return xf.reshape(nb, t_len, D_MODEL), new_states


def kernel(x_prompt, x_sample, c_prompt, c_sample, state_hgrn, state_ssm, state_conv, state_wkv, state_shift,
           w_ada, b_ada, norm1_g, norm2_g, final_g, w_in, hgrn_lb, hgrn_norm_g,
           ssm_conv_w, ssm_conv_b, ssm_dt_bias, ssm_a_log, ssm_d, ssm_norm_g,
           rwkv_mu, rwkv_w0, rwkv_w_w2, rwkv_a0, rwkv_a_w2, rwkv_g_w2, rwkv_k_k, rwkv_k_a, rwkv_r_k,
           rwkv_ln_w, rwkv_ln_b, w_branch, w_out, w_router, b_router, w_gate_e, w_up_e, w_down_e):
    p = {'norm1_g': norm1_g, 'norm2_g': norm2_g, 'final_g': final_g, 'hgrn_lb': hgrn_lb,
         'hgrn_norm_g': hgrn_norm_g, 'ssm_conv_w': ssm_conv_w, 'ssm_conv_b': ssm_conv_b,
         'ssm_dt_bias': ssm_dt_bias, 'ssm_a_log': ssm_a_log, 'ssm_d': ssm_d, 'ssm_norm_g': ssm_norm_g,
         'rwkv_mu': rwkv_mu, 'rwkv_w0': rwkv_w0, 'rwkv_w_w2': rwkv_w_w2, 'rwkv_a0': rwkv_a0,
         'rwkv_a_w2': rwkv_a_w2, 'rwkv_g_w2': rwkv_g_w2, 'rwkv_k_k': rwkv_k_k, 'rwkv_k_a': rwkv_k_a,
         'rwkv_r_k': rwkv_r_k, 'rwkv_ln_w': rwkv_ln_w, 'rwkv_ln_b': rwkv_ln_b}
    depth = w_in.shape[0]
    bp, tp = x_prompt.shape[0], x_prompt.shape[1]
    bs, ts = x_sample.shape[0], x_sample.shape[1]
    wa, wb, wc, wgate = _prep_weights(w_in)
    w_router_pad = jnp.pad(w_router, ((0, 0), (0, LANES - N_EXPERTS)))
    b_router_pad = jnp.pad(b_router.reshape(1, -1), ((0, 0), (0, LANES - N_EXPERTS)))
    weights = (wa, wb, wc, wgate, w_branch.astype(BF16), w_out.astype(BF16),
               w_gate_e.astype(BF16), w_up_e.astype(BF16), w_down_e.astype(BF16),
               w_router_pad, b_router_pad, p)
    ada = _ada(jnp.concatenate([c_prompt, c_sample], axis=0), w_ada, b_ada)

    zeros_p = (jnp.zeros((depth, bp, A_HEADS, A_DK, A_DV), F32),
               jnp.zeros((depth, bp, B_HEADS, B_HEADDIM, B_STATE), F32),
               jnp.zeros((depth, bp, B_CONV - 1, B_CONV_DIM), x_prompt.dtype),
               jnp.zeros((depth, bp, C_HEADS, C_HEADDIM, C_HEADDIM), F32),
               jnp.zeros((depth, bp, 1, C_COLS), x_prompt.dtype))
    tile_p = min(256, tp)
    y_prompt, st_p = _trunk(x_prompt, ada[:, :bp], zeros_p, weights, (tile_p, min(1024, tp)))
    tile_s = min(256, bs * ts)
    y_sample, st_s = _trunk(x_sample, ada[:, bp:], (state_hgrn, state_ssm, state_conv, state_wkv, state_shift),
                            weights, (tile_s, min(512, bs * ts)))
    return (y_prompt, y_sample, *st_p, *st_s)
```

```python
import functools

import jax
import jax.numpy as jnp
from jax import lax
from jax.experimental import pallas as pl
from jax.experimental.pallas import tpu as pltpu

F32 = jnp.float32
BF16 = jnp.bfloat16

D_MODEL = 1024
RMS_EPS = 1e-6
A_HEADS = 4
A_DK = 128
A_DV = 128
A_WIDTH = A_HEADS * A_DV
A_COLS = 4 * A_WIDTH
LB_FLOOR = 1e-30
B_HEADS = 8
B_HEADDIM = 64
B_INNER = B_HEADS * B_HEADDIM
B_GROUPS = 2
B_HPG = B_HEADS // B_GROUPS
B_STATE = 128
B_CONV = 4
B_CONV_DIM = B_INNER + 2 * B_GROUPS * B_STATE
B_COLS = B_INNER + B_CONV_DIM + B_HEADS
C_HEADS = 8
C_HEADDIM = 64
C_WIDTH = C_HEADS * C_HEADDIM
C_DECAY_LORA = 64
C_ICLR_LORA = 64
C_GATE_LORA = 128
C_COLS = 3 * C_WIDTH + C_DECAY_LORA + C_ICLR_LORA + C_GATE_LORA
C_LN_EPS = 64e-5
N_BRANCH = 3
GATE_COLS = N_BRANCH * D_MODEL
MIX_WIDTH = A_WIDTH + B_INNER + C_WIDTH
N_EXPERTS = 16
N_EXPERT_GROUPS = 4
EXPERTS_PER_GROUP = N_EXPERTS // N_EXPERT_GROUPS
D_FF_EXPERT = 512

LANES = 128
SUBLANES = 8
VMEM_LIMIT = 56 * 1024 * 1024

B_DT_PAD = LANES
B_COLS_PAD = B_INNER + B_CONV_DIM + B_DT_PAD
C_LORA_PAD = LANES
C_COLS_PAD = 3 * C_WIDTH + 3 * C_LORA_PAD

TIME_BLOCK = 64


def _nn(a, b):
    return lax.dot_general(a, b, (((1,), (0,)), ((), ())), preferred_element_type=F32)


def _nt(a, b):
    return lax.dot_general(a, b, (((1,), (1,)), ((), ())), preferred_element_type=F32)


def _tn(a, b):
    return lax.dot_general(a, b, (((0,), (0,)), ((), ())), preferred_element_type=F32)


def _split2(x):
    hi = x.astype(BF16)
    lo = (x - hi.astype(F32)).astype(BF16)
    return hi, lo


def _split3(x):
    hi = x.astype(BF16)
    r1 = x - hi.astype(F32)
    mid = r1.astype(BF16)
    lo = (r1 - mid.astype(F32)).astype(BF16)
    return hi, mid, lo


def _mm(dot, a, b, passes):
    if passes == 1:
        return dot(a.astype(BF16), b.astype(BF16))
    a_hi, a_lo = _split2(a)
    b_hi, b_lo = _split2(b)
    return dot(a_hi, b_hi) + (dot(a_hi, b_lo) + dot(a_lo, b_hi))


def _exact_lhs_nn(m_bf16, x):
    hi, mid, lo = _split3(x)
    return _nn(m_bf16, hi) + (_nn(m_bf16, mid) + _nn(m_bf16, lo))


def _exact_rhs_nn(x, m_bf16):
    hi, mid, lo = _split3(x)
    return _nn(hi, m_bf16) + (_nn(mid, m_bf16) + _nn(lo, m_bf16))


def _sigmoid(x):
    return 1.0 / (1.0 + jnp.exp(-x))


def _silu(x):
    return x * _sigmoid(x)


def _softplus(x):
    return jnp.maximum(x, 0.0) + jnp.log1p(jnp.exp(-jnp.abs(x)))


def _iota2(shape, dim):
    return lax.broadcasted_iota(jnp.int32, shape, dim)


def _tril_bf16(n):
    return jnp.where(_iota2((n, n), 1) <= _iota2((n, n), 0), 1.0, 0.0).astype(BF16)


def _norm_mod(x, g, shift, scale, seqs):
    rows, d = x.shape
    y = x * lax.rsqrt(jnp.mean(x * x, axis=-1, keepdims=True) + RMS_EPS) * g
    y = y.reshape(seqs, rows // seqs, d) * (1.0 + scale) + shift
    return y.reshape(rows, d)


def _tile_plan(t_len, tile):
    if tile >= t_len:
        assert tile % t_len == 0
        return tile // t_len, 1
    assert t_len % tile == 0
    return 1, t_len // tile


def _seqs_per_step(nb, want=2):
    return max(d for d in range(1, want + 1) if nb % d == 0)


def _params(*sem):
    return pltpu.CompilerParams(dimension_semantics=sem, vmem_limit_bytes=VMEM_LIMIT)


def _ada_kernel(c_ref, w_ref, b_ref, o_ref):
    s = _silu(c_ref[...])
    o_ref[...] = _mm(_nn, s, w_ref[...], 1) + b_ref[...]


def _ada(c, w_ada, b_ada):
    nb = c.shape[0]
    depth = w_ada.shape[0]
    n_blk = w_ada.shape[2] // D_MODEL
    return pl.pallas_call(
        _ada_kernel,
        grid=(depth, n_blk),
        in_specs=[pl.BlockSpec((nb, D_MODEL), lambda l, j: (0, 0)),
                  pl.BlockSpec((None, D_MODEL, D_MODEL), lambda l, j: (l, 0, j)),
                  pl.BlockSpec((None, 1, D_MODEL), lambda l, j: (l, 0, j))],
        out_specs=pl.BlockSpec((None, nb, D_MODEL), lambda l, j: (l, 0, j)),
        out_shape=jax.ShapeDtypeStruct((depth, nb, w_ada.shape[2]), F32),
        compiler_params=_params("arbitrary", "arbitrary"),
        name="ada",
    )(c, w_ada, b_ada.reshape(depth, 1, -1))


def _inproj_kernel(seqs, x_ref, g_ref, sh_ref, sc_ref, wa_ref, wb_ref, wc_ref, oa_ref, ob_ref, oc_ref):
    h = _norm_mod(x_ref[...], g_ref[...], sh_ref[...], sc_ref[...], seqs).astype(BF16)
    oa_ref[...] = _nn(h, wa_ref[...])
    ob_ref[...] = _nn(h, wb_ref[...])
    oc_ref[...] = _nn(h, wc_ref[...])


def _inproj(x, t_len, g, ada3, wa, wb, wc, tile):
    n = x.shape[0]
    seqs, tps = _tile_plan(t_len, tile)
    full = lambda a: pl.BlockSpec(a.shape, lambda i: (0,) * a.ndim)
    ada_spec = lambda j: pl.BlockSpec((seqs, 1, D_MODEL), lambda i: (i // tps, 0, j))
    row = lambda w: pl.BlockSpec((tile, w), lambda i: (i, 0))
    return pl.pallas_call(
        functools.partial(_inproj_kernel, seqs),
        grid=(n // tile,),
        in_specs=[row(D_MODEL), full(g), ada_spec(0), ada_spec(1), full(wa), full(wb), full(wc)],
        out_specs=[row(A_COLS), row(B_COLS_PAD), row(C_COLS_PAD)],
        out_shape=[jax.ShapeDtypeStruct((n, A_COLS), F32),
                   jax.ShapeDtypeStruct((n, B_COLS_PAD), F32),
                   jax.ShapeDtypeStruct((n, C_COLS_PAD), F32)],
        compiler_params=_params("arbitrary"),
        name="inproj",
    )(x, g, ada3, ada3, wa, wb, wc)


def _hgrn_kernel(layer, cols_ref, lb_ref, ng_ref, s0_ref, o_ref, sout_ref, st_ref, attn_ref):
    t = pl.program_id(1)
    nseq, tb = cols_ref.shape[0], cols_ref.shape[1]
    unit = [(s, h) for s in range(nseq) for h in range(A_HEADS)]

    @pl.when(t == 0)
    def _():
        for s, h in unit:
            st_ref[s, h] = s0_ref[s, h].T

    rows = [lb_ref[i:i + 1, :] for i in range(lb_ref.shape[0])]
    mx = functools.reduce(jnp.maximum, rows)
    es = [jnp.exp(r - mx) for r in rows]
    den = functools.reduce(lambda a, b: a + b, es)
    lower = jnp.zeros_like(mx)
    for i in range(1, layer + 1):
        lower = lower + es[i] / den

    log_lower = jnp.log(jnp.maximum(lower, LB_FLOOR))
    log1m_lower = jnp.log1p(-lower)
    tril = _tril_bf16(tb)

    def prep(s):
        f_pre = cols_ref[s, :, A_WIDTH:2 * A_WIDTH]
        log_sig = jnp.minimum(f_pre, 0.0) - jnp.log1p(jnp.exp(-jnp.abs(f_pre)))
        bterm = log1m_lower + log_sig
        log_f = jnp.maximum(log_lower, bterm) + jnp.log1p(jnp.exp(-jnp.abs(log_lower - bterm)))
        return dict(q=cols_ref[s, :, 0:A_WIDTH], k=(1.0 - lower) / (1.0 + jnp.exp(f_pre)),
                    v=cols_ref[s, :, 2 * A_WIDTH:3 * A_WIDTH].astype(BF16),
                    b=_exact_lhs_nn(tril, log_f))

    seq = [prep(s) for s in range(nseq)]
    units = range(len(unit))
    col = lambda name: [seq[s][name][:, h * A_DK:(h + 1) * A_DK] for s, h in unit]
    q, kf, vh, bh = col('q'), col('k'), col('v'), col('b')
    st = [st_ref[s, h] for s, h in unit]

    blk = SUBLANES
    nblk = tb // blk
    rowblk = _iota2((tb, 1), 0) >> (blk.bit_length() - 1)
    attn_ref[...] = jnp.zeros_like(attn_ref)
    att_off = []
    for u in units:
        k2 = jnp.concatenate([kf[u][j * blk:(j + 1) * blk]
                              * jnp.exp(bh[u][(j + 1) * blk - 1:(j + 1) * blk] - bh[u][j * blk:(j + 1) * blk])
                              for j in range(nblk)], axis=0)
        acc = None
        for j in range(nblk - 1):
            r0 = (j + 1) * blk
            qe = q[u][r0:] * jnp.exp(bh[u][r0:] - bh[u][r0 - 1:r0])
            part = _mm(_nt, qe, jnp.where(rowblk == j, k2, 0.0), 3)
            part = jnp.concatenate([jnp.zeros((r0, tb), F32), part], axis=0)
            acc = part if acc is None else acc + part
        att_off.append(acc)
        s_, h_ = unit[u]
        for sc in range(tb):
            j0 = (sc // blk) * blk
            live = _iota2((blk, 1), 0) >= sc - j0
            diff = bh[u][j0:j0 + blk] - bh[u][sc:sc + 1]
            dec = jnp.where(live, jnp.exp(jnp.where(live, diff, 0.0)), 0.0)
            attn_ref[s_, h_, j0:j0 + blk, sc:sc + 1] = jnp.sum(
                q[u][j0:j0 + blk] * kf[u][sc:sc + 1] * dec, axis=-1, keepdims=True)

    g_all = [cols_ref[s, :, 3 * A_WIDTH:4 * A_WIDTH] for s in range(nseq)]
    for u in units:
        s_, h_ = unit[u]
        hs = slice(h_ * A_DV, (h_ + 1) * A_DV)
        att = attn_ref[s_, h_] + att_off[u]
        oh = (_nt((q[u] * jnp.exp(bh[u])).astype(BF16), st[u].astype(BF16))
              + _nn(att.astype(BF16), vh[u]))
        b_last = bh[u][tb - 1:tb]
        kdec = kf[u] * jnp.exp(b_last - bh[u])
        st_ref[s_, h_] = st[u] * jnp.exp(b_last) + _tn(vh[u], kdec.astype(BF16))
        on = oh * lax.rsqrt(jnp.mean(oh * oh, axis=-1, keepdims=True) + RMS_EPS) * ng_ref[:, hs]
        o_ref[s_, :, hs] = (on * _silu(g_all[s_][:, hs])).astype(o_ref.dtype)

    @pl.when(t == pl.num_programs(1) - 1)
    def _():
        for s, h in unit:
            sout_ref[s, h] = st_ref[s, h].T


def _hgrn(cols_a, nb, t_len, layer, hgrn_lb, norm_g, state0):
    tb = min(TIME_BLOCK, t_len)
    nt = t_len // tb
    nseq = _seqs_per_step(nb)
    state_spec = pl.BlockSpec((nseq, A_HEADS, A_DK, A_DV), lambda b, t: (b, 0, 0, 0))
    o, s_out = pl.pallas_call(
        functools.partial(_hgrn_kernel, layer),
        grid=(nb // nseq, nt),
        in_specs=[pl.BlockSpec((nseq, tb, A_COLS), lambda b, t: (b, t, 0)),
                  pl.BlockSpec(hgrn_lb.shape, lambda b, t: (0, 0)),
                  pl.BlockSpec((1, A_WIDTH), lambda b, t: (0, 0)),
                  state_spec],
        out_specs=[pl.BlockSpec((nseq, tb, A_WIDTH), lambda b, t: (b, t, 0)), state_spec],
        out_shape=[jax.ShapeDtypeStruct((nb, t_len, A_WIDTH), BF16),
                   jax.ShapeDtypeStruct((nb, A_HEADS, A_DK, A_DV), F32)],
        scratch_shapes=[pltpu.VMEM((nseq, A_HEADS, A_DV, A_DK), F32),
                        pltpu.VMEM((nseq, A_HEADS, tb, tb), F32)],
        compiler_params=_params("arbitrary", "arbitrary"),
        name="hgrn2",
    )(cols_a.reshape(nb, t_len, A_COLS), hgrn_lb, norm_g, state0)
    return o.reshape(nb * t_len, A_WIDTH), s_out


def _ssd_kernel(cols_ref, cw_ref, cb_ref, dtb_ref, alog_ref, dsk_ref, ng_ref, cs0_ref, s0_ref,
                o_ref, sout_ref, ext_ref, st_ref):
    t = pl.program_id(1)
    nseq, tb = cols_ref.shape[0], cols_ref.shape[1]
    n_pair = B_HEADS // 2
    pair_w = 2 * B_HEADDIM

    @pl.when(t == 0)
    def _():
        ext_ref[:, 0:SUBLANES, :] = jnp.zeros((nseq, SUBLANES, B_CONV_DIM), F32)
        ext_ref[:, SUBLANES - (B_CONV - 1):SUBLANES, :] = cs0_ref[...]
        st_ref[...] = s0_ref[...]

    rnd = lambda a: a.astype(BF16).astype(F32)
    tril = _tril_bf16(tb)
    causal = _iota2((tb, tb), 1) <= _iota2((tb, tb), 0)
    lane_lo = _iota2((1, pair_w), 1) < B_HEADDIM
    row_lo = _iota2((pair_w, 1), 0) < B_HEADDIM
    neg_a = -jnp.exp(alog_ref[...])
    states = [[st_ref[s, p] for p in range(n_pair)] for s in range(nseq)]
    new_states = []

    for s in range(nseq):
        z = cols_ref[s, :, 0:B_INNER]
        ext_ref[s, SUBLANES:SUBLANES + tb, :] = cols_ref[s, :, B_INNER:B_INNER + B_CONV_DIM]
        conv = None
        for j in range(B_CONV):
            tap = rnd(cw_ref[j:j + 1, :]) * rnd(ext_ref[s, pl.ds(SUBLANES - (B_CONV - 1) + j, tb), :])
            conv = tap if conv is None else conv + tap
        conv = conv + cb_ref[...]
        ext_ref[s, 0:SUBLANES, :] = ext_ref[s, tb:tb + SUBLANES, :]
        xbc = _silu(conv)
        xs = xbc[:, 0:B_INNER]
        bm = xbc[:, B_INNER:B_INNER + B_GROUPS * B_STATE].astype(BF16)
        cm = xbc[:, B_INNER + B_GROUPS * B_STATE:].astype(BF16)

        dt = _softplus(cols_ref[s, :, B_INNER + B_CONV_DIM:] + dtb_ref[...])
        a = _exact_lhs_nn(tril, dt * neg_a)
        a_t = a.T
        dt_t = dt.T
        cb = [_nt(cm[:, g * B_STATE:(g + 1) * B_STATE], bm[:, g * B_STATE:(g + 1) * B_STATE])
              for g in range(B_GROUPS)]

        y_pairs = []
        for p in range(n_pair):
            grp = (2 * p) // B_HPG
            bg = bm[:, grp * B_STATE:(grp + 1) * B_STATE]
            cg = cm[:, grp * B_STATE:(grp + 1) * B_STATE]
            xp = xs[:, p * pair_w:(p + 1) * pair_w]
            sp = states[s][p]
            y_p = _nt(cg, sp.astype(BF16))
            a_cols, w_cols, e_last = [], [], []
            for r in range(2):
                hh = 2 * p + r
                a_col = a[:, hh:hh + 1]
                a_last = a[tb - 1:tb, hh:hh + 1]
                a_cols.append(a_col)
                w_cols.append(jnp.exp(a_last - a_col) * dt[:, hh:hh + 1])
                e_last.append(jnp.exp(a_last))
            y_p = y_p * jnp.where(lane_lo, jnp.exp(a_cols[0]), jnp.exp(a_cols[1]))
            for r in range(2):
                hh = 2 * p + r
                seg = jnp.where(causal, jnp.exp(jnp.where(causal, a_cols[r] - a_t[hh:hh + 1, :], 0.0)), 0.0)
                m = cb[grp] * seg * dt_t[hh:hh + 1, :]
                half = lane_lo if r == 0 else jnp.logical_not(lane_lo)
                y_p = y_p + _nn(m.astype(BF16), jnp.where(half, xp, 0.0).astype(BF16))
            xw = xp * jnp.where(lane_lo, w_cols[0], w_cols[1])
            new_states.append(sp * jnp.where(row_lo, e_last[0], e_last[1]) + _tn(xw.astype(BF16), bg))
            y_pairs.append(y_p)

        y = jnp.concatenate(y_pairs, axis=-1) + dsk_ref[...] * xs
        yz = y * _silu(z)
        o_ref[s] = (yz * lax.rsqrt(jnp.mean(yz * yz, axis=-1, keepdims=True) + RMS_EPS)
                    * ng_ref[...]).astype(o_ref.dtype)

    for s in range(nseq):
        for p in range(n_pair):
            st_ref[s, p] = new_states[s * n_pair + p]

    @pl.when(t == pl.num_programs(1) - 1)
    def _():
        sout_ref[...] = st_ref[...]


def _ssd(cols_b, nb, t_len, conv_w, conv_b, dt_bias, a_log, d_skip, norm_g, conv_state, ssm_state):
    tb = min(TIME_BLOCK, t_len)
    nt = t_len // tb
    n_pair = B_HEADS // 2
    pair_w = 2 * B_HEADDIM
    pad8 = lambda v: jnp.pad(v.reshape(1, B_HEADS), ((0, 0), (0, B_DT_PAD - B_HEADS)))
    const = lambda a: pl.BlockSpec(a.shape, lambda b, t: (0,) * a.ndim)
    args = (conv_w, conv_b.reshape(1, -1), pad8(dt_bias), pad8(a_log),
            jnp.repeat(d_skip, B_HEADDIM).reshape(1, B_INNER), norm_g.reshape(1, -1))
    nseq = _seqs_per_step(nb)
    state_spec = pl.BlockSpec((nseq, n_pair, pair_w, B_STATE), lambda b, t: (b, 0, 0, 0))
    o, s_out = pl.pallas_call(
        _ssd_kernel,
        grid=(nb // nseq, nt),
        in_specs=[pl.BlockSpec((nseq, tb, B_COLS_PAD), lambda b, t: (b, t, 0))]
                 + [const(a) for a in args]
                 + [pl.BlockSpec((nseq, B_CONV - 1, B_CONV_DIM), lambda b, t: (b, 0, 0)), state_spec],
        out_specs=[pl.BlockSpec((nseq, tb, B_INNER), lambda b, t: (b, t, 0)), state_spec],
        out_shape=[jax.ShapeDtypeStruct((nb, t_len, B_INNER), BF16),
                   jax.ShapeDtypeStruct((nb, n_pair, pair_w, B_STATE), F32)],
        scratch_shapes=[pltpu.VMEM((nseq, tb + SUBLANES, B_CONV_DIM), F32),
                        pltpu.VMEM((nseq, n_pair, pair_w, B_STATE), F32)],
        compiler_params=_params("arbitrary", "arbitrary"),
        name="ssd",
    )(cols_b.reshape(nb, t_len, B_COLS_PAD), *args, conv_state, ssm_state.reshape(nb, n_pair, pair_w, B_STATE))
    return o.reshape(nb * t_len, B_INNER), s_out.reshape(nb, B_HEADS, B_HEADDIM, B_STATE)


RWKV_PASSES = 3
RWKV_SEQS_PER_STEP = 4


def _rwkv_kernel(cols_ref, mu_ref, w0_ref, ww2_ref, a0_ref, aw2_ref, gw2_ref, kk_ref, ka_ref, rk_ref,
                 lnw_ref, lnb_ref, sh0_ref, s0_ref, o_ref, sout_ref, ext_ref, st_ref):
    t = pl.program_id(1)
    nseq, tb = cols_ref.shape[0], cols_ref.shape[1]
    n_pair = C_HEADS // 2
    pw = 2 * C_HEADDIM
    s3 = 3 * C_WIDTH
    mm = functools.partial(_mm, passes=RWKV_PASSES)

    @pl.when(t == 0)
    def _():
        ext_ref[:, 0:SUBLANES, :] = jnp.zeros((nseq, SUBLANES, C_COLS_PAD), F32)
        ext_ref[:, SUBLANES - 1:SUBLANES, :] = sh0_ref[...]
        st_ref[...] = s0_ref[...]

    head_of = lambda n, d: _iota2((n, n), d) >> (C_HEADDIM.bit_length() - 1)
    seg_ones = jnp.where(head_of(C_WIDTH, 0) == head_of(C_WIDTH, 1), 1.0, 0.0).astype(BF16)

    def segsum(x):
        hi, lo = _split2(x)
        return _nn(hi, seg_ones) + _nn(lo, seg_ones)

    tril = _tril_bf16(tb)
    rnd = lambda a: a.astype(BF16).astype(F32)

    def prep(s):
        cur = cols_ref[s]
        ext_ref[s, SUBLANES:SUBLANES + tb, :] = cur
        prev = ext_ref[s, pl.ds(SUBLANES - 1, tb), :]
        ext_ref[s, 0:SUBLANES, :] = ext_ref[s, tb:tb + SUBLANES, :]
        mixed = cur + (prev - cur) * mu_ref[...]
        r = mixed[:, 0:C_WIDTH]
        k = mixed[:, C_WIDTH:2 * C_WIDTH]
        v = mixed[:, 2 * C_WIDTH:s3]
        w_lo = mixed[:, s3:s3 + C_LORA_PAD]
        a_lo = mixed[:, s3 + C_LORA_PAD:s3 + 2 * C_LORA_PAD]
        g_lo = mixed[:, s3 + 2 * C_LORA_PAD:]
        w_raw = -_softplus(-(w0_ref[...] + _mm(_nn, jnp.tanh(w_lo), ww2_ref[...], 1))) - 0.5
        logw = -jnp.exp(w_raw)
        a = _sigmoid(a0_ref[...] + _mm(_nn, a_lo, aw2_ref[...], 1))
        g = _mm(_nn, _sigmoid(g_lo), gw2_ref[...], 1)
        kk = k * kk_ref[...]
        kk = kk / jnp.maximum(jnp.sqrt(segsum(kk * kk)), 1e-12)
        k2 = k * (1.0 + (a - 1.0) * ka_ref[...])
        beta = kk * a
        c = _exact_lhs_nn(tril, logw)
        c_last = c[tb - 1:tb, :]
        g_inv = jnp.exp(-c)
        g_hat = jnp.exp(c_last - c)
        return dict(r=r, v=v, g=g, k2=k2, gam_last=jnp.exp(c_last),
                    ab=-rnd(kk) * jnp.exp(c - logw), rb=rnd(r) * jnp.exp(c),
                    bt=beta * g_inv, kt=k2 * g_inv, bh=beta * g_hat, kh=k2 * g_hat)

    seq = [prep(s) for s in range(nseq)]

    n2 = 2 * tb
    ri = _iota2((n2, n2), 0)
    ci = _iota2((n2, n2), 1)
    keep = jnp.where(ri < tb, ci & (tb - 1), (ci & (tb - 1)) - 1) < (ri & (tb - 1))
    eye = jnp.where(ri == ci, 1.0, 0.0)
    left_h = _iota2((tb, n2), 1) < tb
    lane0 = _iota2((1, pw), 1) < C_HEADDIM
    rows_top = _iota2((n2, 1), 0) < tb
    d_own = (rows_top & lane0) | (jnp.logical_not(rows_top) & jnp.logical_not(lane0))
    blockdiag = (_iota2((pw, pw), 0) < C_HEADDIM) == (_iota2((pw, pw), 1) < C_HEADDIM)
    stack = lambda a, b: jnp.concatenate([a, b], axis=0)
    unit = [(s, p) for s in range(nseq) for p in range(n_pair)]
    pairs = range(len(unit))
    col = lambda name: [seq[s][name][:, p * pw:(p + 1) * pw] for s, p in unit]
    sp = [st_ref[s, p] for s, p in unit]
    vp, ab, rb, bt, kt = col('v'), col('ab'), col('rb'), col('bt'), col('kt')
    x = [stack(ab[p], rb[p]) for p in pairs]
    yk = [jnp.concatenate([jnp.where(lane0, bt[p], 0.0), jnp.where(lane0, kt[p], 0.0),
                           jnp.where(lane0, 0.0, kt[p]), jnp.where(lane0, 0.0, bt[p])], axis=0)
          for p in pairs]
    pp = [mm(_nt, x[p], yk[p]) for p in pairs]
    blk0 = [jnp.where(keep, pp[p][:, 0:n2], 0.0) for p in pairs]
    blk1 = [jnp.where(keep, pp[p][:, n2:2 * n2], 0.0) for p in pairs]
    a_bd = [stack(jnp.where(left_h, blk0[p][0:tb], 0.0), jnp.where(left_h, 0.0, blk1[p][0:tb])) for p in pairs]
    q_ak = [stack(jnp.where(left_h, 0.0, blk0[p][0:tb]), jnp.where(left_h, blk1[p][0:tb], 0.0)) for p in pairs]
    r_mat = [stack(blk1[p][tb:n2], blk0[p][tb:n2]) for p in pairs]
    gs = [mm(_nt, x[p], sp[p]) for p in pairs]
    w = [jnp.where(d_own, stack(gs[p][0:tb], gs[p][0:tb]) + mm(_nn, q_ak[p], stack(vp[p], vp[p])), 0.0)
         for p in pairs]
    t_inv = [eye + a_bd[p] for p in pairs]
    a_pow = list(a_bd)
    for _ in range(max(1, (tb - 1).bit_length()) - 1):
        a_pow = [_mm(_nn, a_pow[p], a_pow[p], 1) for p in pairs]
        t_inv = [t_inv[p] + _mm(_nn, t_inv[p], a_pow[p], 1) for p in pairs]
    u = [_mm(_nn, t_inv[p], w[p], 1) for p in pairs]
    res = [w[p] - u[p] + mm(_nn, a_bd[p], u[p]) for p in pairs]
    u = [u[p] + _mm(_nn, t_inv[p], res[p], 1) for p in pairs]
    zz = [u[p] + stack(jnp.where(lane0, 0.0, vp[p]), jnp.where(lane0, vp[p], 0.0)) for p in pairs]
    yf = [jnp.where(d_own, 0.0, mm(_nn, r_mat[p], zz[p]) + stack(gs[p][tb:n2], gs[p][tb:n2])) for p in pairs]
    y_pairs = [yf[p][0:tb] + yf[p][tb:n2] for p in pairs]
    uv = [stack(u[p][0:tb] + u[p][tb:n2], vp[p]) for p in pairs]
    bh, kh, gam = col('bh'), col('kh'), col('gam_last')
    s_new = [sp[p] * gam[p] + jnp.where(blockdiag, mm(_tn, uv[p], stack(bh[p], kh[p])), 0.0) for p in pairs]
    for i, (s, p) in enumerate(unit):
        st_ref[s, p] = s_new[i]

    inv_n = 1.0 / C_HEADDIM
    for s in range(nseq):
        d = seq[s]
        y = jnp.concatenate(y_pairs[s * n_pair:(s + 1) * n_pair], axis=-1)
        mean = segsum(y) * inv_n
        dlt = y - mean
        var = segsum(dlt * dlt) * inv_n
        yn = dlt * lax.rsqrt(var + C_LN_EPS) * lnw_ref[...] + lnb_ref[...]
        bonus = segsum(d['r'] * d['k2'] * rk_ref[...]) * d['v']
        o_ref[s] = ((yn + bonus) * d['g']).astype(o_ref.dtype)

    @pl.when(t == pl.num_programs(1) - 1)
    def _():
        sout_ref[...] = st_ref[...]


def _pad_lora_cols(a):
    s3 = 3 * C_WIDTH
    z = jnp.zeros(a.shape[:-1] + (C_LORA_PAD - C_DECAY_LORA,), a.dtype)
    return jnp.concatenate([a[..., :s3 + C_DECAY_LORA], z,
                            a[..., s3 + C_DECAY_LORA:s3 + C_DECAY_LORA + C_ICLR_LORA], z,
                            a[..., s3 + C_DECAY_LORA + C_ICLR_LORA:]], axis=-1)


def _unpad_lora_cols(a):
    s3 = 3 * C_WIDTH
    return jnp.concatenate([a[..., :s3 + C_DECAY_LORA],
                            a[..., s3 + C_LORA_PAD:s3 + C_LORA_PAD + C_ICLR_LORA],
                            a[..., s3 + 2 * C_LORA_PAD:]], axis=-1)


def _rwkv(cols_c, nb, t_len, shift_state, wkv_state, mu, w0, w_w2, a0, a_w2, g_w2, k_k, k_a, r_k, ln_w, ln_b):
    tb = min(TIME_BLOCK, t_len)
    nt = t_len // tb
    n_pair = C_HEADS // 2
    pw = 2 * C_HEADDIM
    row = lambda v: v.reshape(1, -1)
    pad_rows = lambda w: jnp.pad(w, ((0, C_LORA_PAD - w.shape[0]), (0, 0)))
    args = (row(_pad_lora_cols(mu)), row(w0), pad_rows(w_w2), row(a0), pad_rows(a_w2), g_w2,
            row(k_k), row(k_a), row(r_k), row(ln_w), row(ln_b))
    const = lambda a: pl.BlockSpec(a.shape, lambda b, t: (0,) * a.ndim)
    s4 = wkv_state.reshape(nb, n_pair, 2, C_HEADDIM, C_HEADDIM)
    zero = jnp.zeros_like(s4[:, :, 0])
    s_bd = jnp.concatenate([jnp.concatenate([s4[:, :, 0], zero], axis=-1),
                            jnp.concatenate([zero, s4[:, :, 1]], axis=-1)], axis=-2)
    nseq = _seqs_per_step(nb, RWKV_SEQS_PER_STEP)
    state_spec = pl.BlockSpec((nseq, n_pair, pw, pw), lambda b, t: (b, 0, 0, 0))
    o, s_out = pl.pallas_call(
        _rwkv_kernel,
        grid=(nb // nseq, nt),
        in_specs=[pl.BlockSpec((nseq, tb, C_COLS_PAD), lambda b, t: (b, t, 0))]
                 + [const(a) for a in args]
                 + [pl.BlockSpec((nseq, 1, C_COLS_PAD), lambda b, t: (b, 0, 0)), state_spec],
        out_specs=[pl.BlockSpec((nseq, tb, C_WIDTH), lambda b, t: (b, t, 0)), state_spec],
        out_shape=[jax.ShapeDtypeStruct((nb, t_len, C_WIDTH), BF16),
                   jax.ShapeDtypeStruct((nb, n_pair, pw, pw), F32)],
        scratch_shapes=[pltpu.VMEM((nseq, tb + SUBLANES, C_COLS_PAD), F32),
                        pltpu.VMEM((nseq, n_pair, pw, pw), F32)],
        compiler_params=_params("arbitrary", "arbitrary"),
        name="rwkv7",
    )(cols_c.reshape(nb, t_len, C_COLS_PAD), *args, _pad_lora_cols(shift_state), s_bd)
    s_new = jnp.stack([s_out[:, :, :C_HEADDIM, :C_HEADDIM], s_out[:, :, C_HEADDIM:, C_HEADDIM:]], axis=2)
    return o.reshape(nb * t_len, C_WIDTH), s_new.reshape(nb, C_HEADS, C_HEADDIM, C_HEADDIM)


def _merge_kernel(seqs, x_ref, g_ref, sh_ref, sc_ref, gt_ref, oa_ref, ob_ref, oc_ref,
                  wg_ref, wb_ref, wo_ref, out_ref):
    x = x_ref[...]
    rows = x.shape[0]
    h = _norm_mod(x, g_ref[...], sh_ref[...], sc_ref[...], seqs).astype(BF16)
    merged = jnp.zeros((rows, D_MODEL), F32)
    off = 0
    for i, o_ref in enumerate((oa_ref, ob_ref, oc_ref)):
        width = o_ref.shape[1]
        gate = _sigmoid(_nn(h, wg_ref[:, i * D_MODEL:(i + 1) * D_MODEL]))
        merged = merged + gate * _nn(o_ref[...], wb_ref[off:off + width, :])
        off += width
    mix = _nn(merged.astype(BF16), wo_ref[...])
    out = x.reshape(seqs, rows // seqs, D_MODEL) + gt_ref[...] * mix.reshape(seqs, rows // seqs, D_MODEL)
    out_ref[...] = out.reshape(rows, D_MODEL)


def _merge(x, t_len, g, ada3, o_a, o_b, o_c, wg, wb, wo, tile):
    n = x.shape[0]
    seqs, tps = _tile_plan(t_len, tile)
    full = lambda a: pl.BlockSpec(a.shape, lambda i: (0,) * a.ndim)
    ada_spec = lambda j: pl.BlockSpec((seqs, 1, D_MODEL), lambda i: (i // tps, 0, j))
    row = lambda w: pl.BlockSpec((tile, w), lambda i: (i, 0))
    return pl.pallas_call(
        functools.partial(_merge_kernel, seqs),
        grid=(n // tile,),
        in_specs=[row(D_MODEL), full(g), ada_spec(0), ada_spec(1), ada_spec(2),
                  row(A_WIDTH), row(B_INNER), row(C_WIDTH), full(wg), full(wb), full(wo)],
        out_specs=row(D_MODEL),
        out_shape=jax.ShapeDtypeStruct((n, D_MODEL), F32),
        compiler_params=_params("arbitrary"),
        name="merge",
    )(x, g, ada3, ada3, ada3, o_a, o_b, o_c, wg, wb, wo)


def _route(scores, bias):
    lane = _iota2(scores.shape, 1)
    sel = scores + bias

    def partner(x, d, span):
        wrap = (lane & (span - 1)) + d >= span
        fwd = pltpu.roll(x, x.shape[1] - d, 1)
        back = pltpu.roll(x, span - d, 1)
        return jnp.where(wrap, back, fwd), wrap

    rank = jnp.zeros(scores.shape, jnp.int32)
    for d in range(1, EXPERTS_PER_GROUP):
        other, wrap = partner(sel, d, EXPERTS_PER_GROUP)
        beats = (other > sel) | (wrap & (other == sel))
        rank = rank + jnp.where(beats, 1, 0)
    in_top = rank < 2
    g_score = jnp.where(in_top, sel, 0.0)
    top_s = jnp.where(in_top, scores, 0.0)
    g_sum, w_sum = g_score, top_s
    for d in range(1, EXPERTS_PER_GROUP):
        g_sum = g_sum + partner(g_score, d, EXPERTS_PER_GROUP)[0]
        w_sum = w_sum + partner(top_s, d, EXPERTS_PER_GROUP)[0]
    chosen = in_top & (lane < N_EXPERTS)
    for d in range(EXPERTS_PER_GROUP, N_EXPERTS, EXPERTS_PER_GROUP):
        other, wrap = partner(g_sum, d, N_EXPERTS)
        chosen = chosen & ((other < g_sum) | (jnp.logical_not(wrap) & (other == g_sum)))
    return jnp.where(chosen, scores / w_sum, 0.0), chosen


MOE_ROW_BLOCK = 128
MOE_EXPERTS_PER_STEP = 2


def _expert_out(xb, c_e, wg, wu, wd):
    hid = _silu(_nn(xb, wg)) * _nn(xb, wu)
    y = _nn(hid.astype(BF16), wd)
    return jnp.where(c_e != 0.0, c_e * y, 0.0)


def _lane_pick(a, lane_idx):
    return jnp.sum(jnp.where(_iota2(a.shape, 1) == lane_idx, a, 0.0), axis=-1, keepdims=True)


def _residual_out(seqs, final, x, moe, gate, final_g):
    rows = x.shape[0]
    out = x.reshape(seqs, rows // seqs, D_MODEL) + gate * moe.reshape(seqs, rows // seqs, D_MODEL)
    out = out.reshape(rows, D_MODEL)
    if final:
        out = out * lax.rsqrt(jnp.mean(out * out, axis=-1, keepdims=True) + RMS_EPS) * final_g
    return out


def _moe_kernel(seqs, final, x_ref, g_ref, sh_ref, sc_ref, gt_ref, wr_ref, br_ref, fg_ref,
                wg_ref, wu_ref, wd_ref, out_ref, xs_ref, cc_ref, pt_ref, acc_ref, seg_ref):
    step = pl.program_id(1)
    per_step = wg_ref.shape[0]
    rows = x_ref.shape[0]
    ncomp = xs_ref.shape[0]
    lanes = cc_ref.shape[1]
    rb = MOE_ROW_BLOCK

    @pl.when(step == 0)
    def _():
        h = _norm_mod(x_ref[...], g_ref[...], sh_ref[...], sc_ref[...], seqs)
        scores = _sigmoid(_mm(_nn, h, wr_ref[...], 1))
        coef, chosen = _route(scores, br_ref[...])
        shift = EXPERTS_PER_GROUP.bit_length() - 1
        li, lj = _iota2((lanes, lanes), 0), _iota2((lanes, lanes), 1)
        gmat = jnp.where(((li >> shift) == lj) & (li < N_EXPERTS), 0.5, 0.0).astype(BF16)
        member = _nn(jnp.where(chosen, 1.0, 0.0).astype(BF16), gmat)
        before = jnp.where(_iota2((rows, rows), 1) < _iota2((rows, rows), 0), 1.0, 0.0).astype(BF16)
        rank = _nn(before, member.astype(BF16))
        count = rank[rows - 1:rows, :] + member[rows - 1:rows, :]
        blocks = jnp.floor((count + (rb - 1)) * (1.0 / rb))
        earlier = jnp.where(li < lj, 1.0, 0.0).astype(BF16)
        start = _nn(jnp.broadcast_to(blocks * rb, (SUBLANES, lanes)).astype(BF16), earlier)[0:1, :]
        lane_row = _iota2((1, lanes), 1)
        for gi in range(N_EXPERT_GROUPS):
            seg_ref[gi] = jnp.sum(jnp.where(lane_row == gi, start, 0.0)).astype(jnp.int32)
            seg_ref[N_EXPERT_GROUPS + gi] = jnp.sum(jnp.where(lane_row == gi, blocks, 0.0)).astype(jnp.int32)
        dest = jnp.sum(member * (start + rank), axis=-1, keepdims=True)
        dest_row = jnp.broadcast_to(dest, (rows, lanes)).T[0:1, :].astype(jnp.int32)
        perm = jnp.where(_iota2((ncomp, rows), 0) == dest_row, 1.0, 0.0).astype(BF16)
        xs_ref[...] = _nn(perm, h.astype(BF16)).astype(BF16)
        c_hi, c_lo = _split2(coef)
        cc_ref[...] = _nn(perm, c_hi) + _nn(perm, c_lo)
        pt_ref[...] = jnp.where(_iota2((rows, ncomp), 1) == dest.astype(jnp.int32), 1.0, 0.0).astype(BF16)
        acc_ref[...] = jnp.zeros_like(acc_ref)

    grp = (step * per_step) // EXPERTS_PER_GROUP

    def block(k, carry):
        r0 = pl.multiple_of(seg_ref[grp] + k * rb, rb)
        xb = xs_ref[pl.ds(r0, rb), :]
        cc = cc_ref[pl.ds(r0, rb), :]
        upd = None
        for j in range(per_step):
            y = _expert_out(xb, _lane_pick(cc, step * per_step + j), wg_ref[j], wu_ref[j], wd_ref[j])
            upd = y if upd is None else upd + y
        acc_ref[pl.ds(r0, rb), :] += upd
        return carry

    lax.fori_loop(0, seg_ref[N_EXPERT_GROUPS + grp], block, 0)

    @pl.when(step == pl.num_programs(1) - 1)
    def _():
        a_hi, a_lo = _split2(acc_ref[...])
        moe = _nn(pt_ref[...], a_hi) + _nn(pt_ref[...], a_lo)
        out_ref[...] = _residual_out(seqs, final, x_ref[...], moe, gt_ref[...], fg_ref[...])


def _moe(x, t_len, g, ada3, w_router, b_router, final_g, wg, wu, wd, tile, final):
    n = x.shape[0]
    seqs, tps = _tile_plan(t_len, tile)
    ncomp = tile + N_EXPERT_GROUPS * MOE_ROW_BLOCK
    full = lambda a: pl.BlockSpec(a.shape, lambda i, e: (0,) * a.ndim)
    ada_spec = lambda j: pl.BlockSpec((seqs, 1, D_MODEL), lambda i, e: (i // tps, 0, j))
    row = pl.BlockSpec((tile, D_MODEL), lambda i, e: (i, 0))
    assert EXPERTS_PER_GROUP % MOE_EXPERTS_PER_STEP == 0
    expert = lambda a: pl.BlockSpec((MOE_EXPERTS_PER_STEP,) + a.shape[1:], lambda i, e: (e, 0, 0))
    return pl.pallas_call(
        functools.partial(_moe_kernel, seqs, final),
        grid=(n // tile, N_EXPERTS // MOE_EXPERTS_PER_STEP),
        in_specs=[row, full(g), ada_spec(3), ada_spec(4), ada_spec(5), full(w_router), full(b_router),
                  full(final_g), expert(wg), expert(wu), expert(wd)],
        out_specs=row,
        out_shape=jax.ShapeDtypeStruct((n, D_MODEL), F32),
        scratch_shapes=[pltpu.VMEM((ncomp, D_MODEL), BF16),
                        pltpu.VMEM((ncomp, LANES), F32),
                        pltpu.VMEM((tile, ncomp), BF16),
                        pltpu.VMEM((ncomp, D_MODEL), F32),
                        pltpu.SMEM((2 * N_EXPERT_GROUPS,), jnp.int32)],
        compiler_params=_params("arbitrary", "arbitrary"),
        name="moe",
    )(x, g, ada3, ada3, ada3, w_router, b_router, final_g, wg, wu, wd)


def _prep_weights(w_in):
    o_b = A_COLS
    o_c = A_COLS + B_COLS
    o_g = o_c + C_COLS
    wa = w_in[..., :o_b]
    wb = jnp.pad(w_in[..., o_b:o_c], ((0, 0), (0, 0), (0, B_DT_PAD - B_HEADS)))
    wc = _pad_lora_cols(w_in[..., o_c:o_g])
    wgate = w_in[..., o_g:]
    return tuple(w.astype(BF16) for w in (wa, wb, wc, wgate))


def _trunk(x, ada, states, weights, tiles):
    nb, t_len, _ = x.shape
    st_hgrn, st_ssm, st_conv, st_wkv, st_shift = states
    (wa, wb, wc, wgate, w_branch, w_out, w_gate_e, w_up_e, w_down_e, w_router, b_router, p) = weights
    tile_tok, tile_moe = tiles
    depth = wa.shape[0]
    xf = x.reshape(nb * t_len, D_MODEL)
    row = lambda v: v.reshape(1, -1)
    outs = []
    for l in range(depth):
        ada3 = ada[l].reshape(nb, 1, -1)
        g1 = row(p['norm1_g'][l])
        cols_a, cols_b, cols_c = _inproj(xf, t_len, g1, ada3, wa[l], wb[l], wc[l], tile_tok)
        o_a, hgrn_new = _hgrn(cols_a, nb, t_len, l, p['hgrn_lb'], row(p['hgrn_norm_g'][l]), st_hgrn[l])
        o_b, ssm_new = _ssd(cols_b, nb, t_len, p['ssm_conv_w'][l], p['ssm_conv_b'][l], p['ssm_dt_bias'][l],
                            p['ssm_a_log'][l], p['ssm_d'][l], p['ssm_norm_g'][l], st_conv[l], st_ssm[l])
        o_c, wkv_new = _rwkv(cols_c, nb, t_len, st_shift[l], st_wkv[l], p['rwkv_mu'][l], p['rwkv_w0'][l],
                             p['rwkv_w_w2'][l], p['rwkv_a0'][l], p['rwkv_a_w2'][l], p['rwkv_g_w2'][l],
                             p['rwkv_k_k'][l], p['rwkv_k_a'][l], p['rwkv_r_k'][l].reshape(-1),
                             p['rwkv_ln_w'][l], p['rwkv_ln_b'][l])
        assert t_len >= B_CONV - 1
        conv_new = cols_b.reshape(nb, t_len, B_COLS_PAD)[:, t_len - (B_CONV - 1):, B_INNER:B_INNER + B_CONV_DIM]
        shift_new = _unpad_lora_cols(cols_c.reshape(nb, t_len, C_COLS_PAD)[:, t_len - 1:])
        xf = _merge(xf, t_len, g1, ada3, o_a, o_b, o_c, wgate[l], w_branch[l], w_out[l], tile_tok)
        xf = _moe(xf, t_len, row(p['norm2_g'][l]), ada3, w_router, b_router, row(p['final_g']),
                  w_gate_e[l], w_up_e[l], w_down_e[l], tile_moe, final=(l == depth - 1))
        outs.append((hgrn_new, ssm_new, conv_new, wkv_new, shift_new))
    new_states = [jnp.stack(s, axis=0) for s in zip(*outs)]
    return xf.reshape(nb, t_len, D_MODEL), new_states


def kernel(x_prompt, x_sample, c_prompt, c_sample, state_hgrn, state_ssm, state_conv, state_wkv, state_shift,
           w_ada, b_ada, norm1_g, norm2_g, final_g, w_in, hgrn_lb, hgrn_norm_g,
           ssm_conv_w, ssm_conv_b, ssm_dt_bias, ssm_a_log, ssm_d, ssm_norm_g,
           rwkv_mu, rwkv_w0, rwkv_w_w2, rwkv_a0, rwkv_a_w2, rwkv_g_w2, rwkv_k_k, rwkv_k_a, rwkv_r_k,
           rwkv_ln_w, rwkv_ln_b, w_branch, w_out, w_router, b_router, w_gate_e, w_up_e, w_down_e):
    p = {'norm1_g': norm1_g, 'norm2_g': norm2_g, 'final_g': final_g, 'hgrn_lb': hgrn_lb,
         'hgrn_norm_g': hgrn_norm_g, 'ssm_conv_w': ssm_conv_w, 'ssm_conv_b': ssm_conv_b,
         'ssm_dt_bias': ssm_dt_bias, 'ssm_a_log': ssm_a_log, 'ssm_d': ssm_d, 'ssm_norm_g': ssm_norm_g,
         'rwkv_mu': rwkv_mu, 'rwkv_w0': rwkv_w0, 'rwkv_w_w2': rwkv_w_w2, 'rwkv_a0': rwkv_a0,
         'rwkv_a_w2': rwkv_a_w2, 'rwkv_g_w2': rwkv_g_w2, 'rwkv_k_k': rwkv_k_k, 'rwkv_k_a': rwkv_k_a,
         'rwkv_r_k': rwkv_r_k, 'rwkv_ln_w': rwkv_ln_w, 'rwkv_ln_b': rwkv_ln_b}
    depth = w_in.shape[0]
    bp, tp = x_prompt.shape[0], x_prompt.shape[1]
    bs, ts = x_sample.shape[0], x_sample.shape[1]
    wa, wb, wc, wgate = _prep_weights(w_in)
    w_router_pad = jnp.pad(w_router, ((0, 0), (0, LANES - N_EXPERTS)))
    b_router_pad = jnp.pad(b_router.reshape(1, -1), ((0, 0), (0, LANES - N_EXPERTS)))
    weights = (wa, wb, wc, wgate, w_branch.astype(BF16), w_out.astype(BF16),
               w_gate_e.astype(BF16), w_up_e.astype(BF16), w_down_e.astype(BF16),
               w_router_pad, b_router_pad, p)
    ada = _ada(jnp.concatenate([c_prompt, c_sample], axis=0), w_ada, b_ada)

    zeros_p = (jnp.zeros((depth, bp, A_HEADS, A_DK, A_DV), F32),
               jnp.zeros((depth, bp, B_HEADS, B_HEADDIM, B_STATE), F32),
               jnp.zeros((depth, bp, B_CONV - 1, B_CONV_DIM), x_prompt.dtype),
               jnp.zeros((depth, bp, C_HEADS, C_HEADDIM, C_HEADDIM), F32),
               jnp.zeros((depth, bp, 1, C_COLS), x_prompt.dtype))
    tile_p = min(256, tp)
    y_prompt, st_p = _trunk(x_prompt, ada[:, :bp], zeros_p, weights, (tile_p, min(1024, tp)))
    tile_s = min(256, bs * ts)
    y_sample, st_s = _trunk(x_sample, ada[:, bp:], (state_hgrn, state_ssm, state_conv, state_wkv, state_shift),
                            weights, (tile_s, min(512, bs * ts)))
    return (y_prompt, y_sample, *st_p, *st_s)
```

```python
import functools

import jax
import jax.numpy as jnp
from jax import lax
from jax.experimental import pallas as pl
from jax.experimental.pallas import tpu as pltpu

F32 = jnp.float32
BF16 = jnp.bfloat16

D_MODEL = 1024
RMS_EPS = 1e-6
A_HEADS = 4
A_DK = 128
A_DV = 128
A_WIDTH = A_HEADS * A_DV
A_COLS = 4 * A_WIDTH
LB_FLOOR = 1e-30
B_HEADS = 8
B_HEADDIM = 64
B_INNER = B_HEADS * B_HEADDIM
B_GROUPS = 2
B_HPG = B_HEADS // B_GROUPS
B_STATE = 128
B_CONV = 4
B_CONV_DIM = B_INNER + 2 * B_GROUPS * B_STATE
B_COLS = B_INNER + B_CONV_DIM + B_HEADS
C_HEADS = 8
C_HEADDIM = 64
C_WIDTH = C_HEADS * C_HEADDIM
C_DECAY_LORA = 64
C_ICLR_LORA = 64
C_GATE_LORA = 128
C_COLS = 3 * C_WIDTH + C_DECAY_LORA + C_ICLR_LORA + C_GATE_LORA
C_LN_EPS = 64e-5
N_BRANCH = 3
GATE_COLS = N_BRANCH * D_MODEL
MIX_WIDTH = A_WIDTH + B_INNER + C_WIDTH
N_EXPERTS = 16
N_EXPERT_GROUPS = 4
EXPERTS_PER_GROUP = N_EXPERTS // N_EXPERT_GROUPS
D_FF_EXPERT = 512

LANES = 128
SUBLANES = 8
VMEM_LIMIT = 56 * 1024 * 1024

B_DT_PAD = LANES
B_COLS_PAD = B_INNER + B_CONV_DIM + B_DT_PAD
C_LORA_PAD = LANES
C_COLS_PAD = 3 * C_WIDTH + 3 * C_LORA_PAD

TIME_BLOCK = 64


def _nn(a, b):
    return lax.dot_general(a, b, (((1,), (0,)), ((), ())), preferred_element_type=F32)


def _nt(a, b):
    return lax.dot_general(a, b, (((1,), (1,)), ((), ())), preferred_element_type=F32)


def _tn(a, b):
    return lax.dot_general(a, b, (((0,), (0,)), ((), ())), preferred_element_type=F32)


def _split2(x):
    hi = x.astype(BF16)
    lo = (x - hi.astype(F32)).astype(BF16)
    return hi, lo


def _split3(x):
    hi = x.astype(BF16)
    r1 = x - hi.astype(F32)
    mid = r1.astype(BF16)
    lo = (r1 - mid.astype(F32)).astype(BF16)
    return hi, mid, lo


def _mm(dot, a, b, passes):
    if passes == 1:
        return dot(a.astype(BF16), b.astype(BF16))
    a_hi, a_lo = _split2(a)
    b_hi, b_lo = _split2(b)
    return dot(a_hi, b_hi) + (dot(a_hi, b_lo) + dot(a_lo, b_hi))


def _exact_lhs_nn(m_bf16, x):
    hi, mid, lo = _split3(x)
    return _nn(m_bf16, hi) + (_nn(m_bf16, mid) + _nn(m_bf16, lo))


def _exact_rhs_nn(x, m_bf16):
    hi, mid, lo = _split3(x)
    return _nn(hi, m_bf16) + (_nn(mid, m_bf16) + _nn(lo, m_bf16))


def _sigmoid(x):
    return 1.0 / (1.0 + jnp.exp(-x))


def _silu(x):
    return x * _sigmoid(x)


def _softplus(x):
    return jnp.maximum(x, 0.0) + jnp.log1p(jnp.exp(-jnp.abs(x)))


def _iota2(shape, dim):
    return lax.broadcasted_iota(jnp.int32, shape, dim)


def _tril_bf16(n):
    return jnp.where(_iota2((n, n), 1) <= _iota2((n, n), 0), 1.0, 0.0).astype(BF16)


def _norm_mod(x, g, shift, scale, seqs):
    rows, d = x.shape
    y = x * lax.rsqrt(jnp.mean(x * x, axis=-1, keepdims=True) + RMS_EPS) * g
    y = y.reshape(seqs, rows // seqs, d) * (1.0 + scale) + shift
    return y.reshape(rows, d)


def _tile_plan(t_len, tile):
    if tile >= t_len:
        assert tile % t_len == 0
        return tile // t_len, 1
    assert t_len % tile == 0
    return 1, t_len // tile


def _seqs_per_step(nb, want=2):
    return max(d for d in range(1, want + 1) if nb % d == 0)


def _params(*sem):
    return pltpu.CompilerParams(dimension_semantics=sem, vmem_limit_bytes=VMEM_LIMIT)


def _ada_kernel(c_ref, w_ref, b_ref, o_ref):
    s = _silu(c_ref[...])
    o_ref[...] = _mm(_nn, s, w_ref[...], 1) + b_ref[...]


def _ada(c, w_ada, b_ada):
    nb = c.shape[0]
    depth = w_ada.shape[0]
    n_blk = w_ada.shape[2] // D_MODEL
    return pl.pallas_call(
        _ada_kernel,
        grid=(depth, n_blk),
        in_specs=[pl.BlockSpec((nb, D_MODEL), lambda l, j: (0, 0)),
                  pl.BlockSpec((None, D_MODEL, D_MODEL), lambda l, j: (l, 0, j)),
                  pl.BlockSpec((None, 1, D_MODEL), lambda l, j: (l, 0, j))],
        out_specs=pl.BlockSpec((None, nb, D_MODEL), lambda l, j: (l, 0, j)),
        out_shape=jax.ShapeDtypeStruct((depth, nb, w_ada.shape[2]), F32),
        compiler_params=_params("arbitrary", "arbitrary"),
        name="ada",
    )(c, w_ada, b_ada.reshape(depth, 1, -1))


def _inproj_kernel(seqs, x_ref, g_ref, sh_ref, sc_ref, wa_ref, wb_ref, wc_ref, oa_ref, ob_ref, oc_ref):
    h = _norm_mod(x_ref[...], g_ref[...], sh_ref[...], sc_ref[...], seqs).astype(BF16)
    oa_ref[...] = _nn(h, wa_ref[...])
    ob_ref[...] = _nn(h, wb_ref[...])
    oc_ref[...] = _nn(h, wc_ref[...])


def _inproj(x, t_len, g, ada3, wa, wb, wc, tile):
    n = x.shape[0]
    seqs, tps = _tile_plan(t_len, tile)
    full = lambda a: pl.BlockSpec(a.shape, lambda i: (0,) * a.ndim)
    ada_spec = lambda j: pl.BlockSpec((seqs, 1, D_MODEL), lambda i: (i // tps, 0, j))
    row = lambda w: pl.BlockSpec((tile, w), lambda i: (i, 0))
    return pl.pallas_call(
        functools.partial(_inproj_kernel, seqs),
        grid=(n // tile,),
        in_specs=[row(D_MODEL), full(g), ada_spec(0), ada_spec(1), full(wa), full(wb), full(wc)],
        out_specs=[row(A_COLS), row(B_COLS_PAD), row(C_COLS_PAD)],
        out_shape=[jax.ShapeDtypeStruct((n, A_COLS), F32),
                   jax.ShapeDtypeStruct((n, B_COLS_PAD), F32),
                   jax.ShapeDtypeStruct((n, C_COLS_PAD), F32)],
        compiler_params=_params("arbitrary"),
        name="inproj",
    )(x, g, ada3, ada3, wa, wb, wc)


def _hgrn_kernel(layer, cols_ref, lb_ref, ng_ref, s0_ref, o_ref, sout_ref, st_ref, attn_ref):
    t = pl.program_id(1)
    nseq, tb = cols_ref.shape[0], cols_ref.shape[1]
    unit = [(s, h) for s in range(nseq) for h in range(A_HEADS)]

    @pl.when(t == 0)
    def _():
        for s, h in unit:
            st_ref[s, h] = s0_ref[s, h].T

    rows = [lb_ref[i:i + 1, :] for i in range(lb_ref.shape[0])]
    mx = functools.reduce(jnp.maximum, rows)
    es = [jnp.exp(r - mx) for r in rows]
    den = functools.reduce(lambda a, b: a + b, es)
    lower = jnp.zeros_like(mx)
    for i in range(1, layer + 1):
        lower = lower + es[i] / den

    log_lower = jnp.log(jnp.maximum(lower, LB_FLOOR))
    log1m_lower = jnp.log1p(-lower)
    tril = _tril_bf16(tb)

    def prep(s):
        f_pre = cols_ref[s, :, A_WIDTH:2 * A_WIDTH]
        log_sig = jnp.minimum(f_pre, 0.0) - jnp.log1p(jnp.exp(-jnp.abs(f_pre)))
        bterm = log1m_lower + log_sig
        log_f = jnp.maximum(log_lower, bterm) + jnp.log1p(jnp.exp(-jnp.abs(log_lower - bterm)))
        return dict(q=cols_ref[s, :, 0:A_WIDTH], k=(1.0 - lower) / (1.0 + jnp.exp(f_pre)),
                    v=cols_ref[s, :, 2 * A_WIDTH:3 * A_WIDTH].astype(BF16),
                    b=_exact_lhs_nn(tril, log_f))

    seq = [prep(s) for s in range(nseq)]
    units = range(len(unit))
    col = lambda name: [seq[s][name][:, h * A_DK:(h + 1) * A_DK] for s, h in unit]
    q, kf, vh, bh = col('q'), col('k'), col('v'), col('b')
    st = [st_ref[s, h] for s, h in unit]

    blk = SUBLANES
    nblk = tb // blk
    rowblk = _iota2((tb, 1), 0) >> (blk.bit_length() - 1)
    attn_ref[...] = jnp.zeros_like(attn_ref)
    att_off = []
    for u in units:
        k2 = jnp.concatenate([kf[u][j * blk:(j + 1) * blk]
                              * jnp.exp(bh[u][(j + 1) * blk - 1:(j + 1) * blk] - bh[u][j * blk:(j + 1) * blk])
                              for j in range(nblk)], axis=0)
        acc = None
        for j in range(nblk - 1):
            r0 = (j + 1) * blk
            qe = q[u][r0:] * jnp.exp(bh[u][r0:] - bh[u][r0 - 1:r0])
            part = _mm(_nt, qe, jnp.where(rowblk == j, k2, 0.0), 3)
            part = jnp.concatenate([jnp.zeros((r0, tb), F32), part], axis=0)
            acc = part if acc is None else acc + part
        att_off.append(acc)
        s_, h_ = unit[u]
        for sc in range(tb):
            j0 = (sc // blk) * blk
            live = _iota2((blk, 1), 0) >= sc - j0
            diff = bh[u][j0:j0 + blk] - bh[u][sc:sc + 1]
            dec = jnp.where(live, jnp.exp(jnp.where(live, diff, 0.0)), 0.0)
            attn_ref[s_, h_, j0:j0 + blk, sc:sc + 1] = jnp.sum(
                q[u][j0:j0 + blk] * kf[u][sc:sc + 1] * dec, axis=-1, keepdims=True)

    g_all = [cols_ref[s, :, 3 * A_WIDTH:4 * A_WIDTH] for s in range(nseq)]
    for u in units:
        s_, h_ = unit[u]
        hs = slice(h_ * A_DV, (h_ + 1) * A_DV)
        att = attn_ref[s_, h_] + att_off[u]
        oh = (_nt((q[u] * jnp.exp(bh[u])).astype(BF16), st[u].astype(BF16))
              + _nn(att.astype(BF16), vh[u]))
        b_last = bh[u][tb - 1:tb]
        kdec = kf[u] * jnp.exp(b_last - bh[u])
        st_ref[s_, h_] = st[u] * jnp.exp(b_last) + _tn(vh[u], kdec.astype(BF16))
        on = oh * lax.rsqrt(jnp.mean(oh * oh, axis=-1, keepdims=True) + RMS_EPS) * ng_ref[:, hs]
        o_ref[s_, :, hs] = (on * _silu(g_all[s_][:, hs])).astype(o_ref.dtype)

    @pl.when(t == pl.num_programs(1) - 1)
    def _():
        for s, h in unit:
            sout_ref[s, h] = st_ref[s, h].T


def _hgrn(cols_a, nb, t_len, layer, hgrn_lb, norm_g, state0):
    tb = min(TIME_BLOCK, t_len)
    nt = t_len // tb
    nseq = _seqs_per_step(nb)
    state_spec = pl.BlockSpec((nseq, A_HEADS, A_DK, A_DV), lambda b, t: (b, 0, 0, 0))
    o, s_out = pl.pallas_call(
        functools.partial(_hgrn_kernel, layer),
        grid=(nb // nseq, nt),
        in_specs=[pl.BlockSpec((nseq, tb, A_COLS), lambda b, t: (b, t, 0)),
                  pl.BlockSpec(hgrn_lb.shape, lambda b, t: (0, 0)),
                  pl.BlockSpec((1, A_WIDTH), lambda b, t: (0, 0)),
                  state_spec],
        out_specs=[pl.BlockSpec((nseq, tb, A_WIDTH), lambda b, t: (b, t, 0)), state_spec],
        out_shape=[jax.ShapeDtypeStruct((nb, t_len, A_WIDTH), BF16),
                   jax.ShapeDtypeStruct((nb, A_HEADS, A_DK, A_DV), F32)],
        scratch_shapes=[pltpu.VMEM((nseq, A_HEADS, A_DV, A_DK), F32),
                        pltpu.VMEM((nseq, A_HEADS, tb, tb), F32)],
        compiler_params=_params("arbitrary", "arbitrary"),
        name="hgrn2",
    )(cols_a.reshape(nb, t_len, A_COLS), hgrn_lb, norm_g, state0)
    return o.reshape(nb * t_len, A_WIDTH), s_out


def _ssd_kernel(cols_ref, cw_ref, cb_ref, dtb_ref, alog_ref, dsk_ref, ng_ref, cs0_ref, s0_ref,
                o_ref, sout_ref, ext_ref, st_ref):
    t = pl.program_id(1)
    nseq, tb = cols_ref.shape[0], cols_ref.shape[1]
    n_pair = B_HEADS // 2
    pair_w = 2 * B_HEADDIM

    @pl.when(t == 0)
    def _():
        ext_ref[:, 0:SUBLANES, :] = jnp.zeros((nseq, SUBLANES, B_CONV_DIM), F32)
        ext_ref[:, SUBLANES - (B_CONV - 1):SUBLANES, :] = cs0_ref[...]
        st_ref[...] = s0_ref[...]

    rnd = lambda a: a.astype(BF16).astype(F32)
    tril = _tril_bf16(tb)
    causal = _iota2((tb, tb), 1) <= _iota2((tb, tb), 0)
    lane_lo = _iota2((1, pair_w), 1) < B_HEADDIM
    row_lo = _iota2((pair_w, 1), 0) < B_HEADDIM
    neg_a = -jnp.exp(alog_ref[...])
    states = [[st_ref[s, p] for p in range(n_pair)] for s in range(nseq)]
    new_states = []

    for s in range(nseq):
        z = cols_ref[s, :, 0:B_INNER]
        ext_ref[s, SUBLANES:SUBLANES + tb, :] = cols_ref[s, :, B_INNER:B_INNER + B_CONV_DIM]
        conv = None
        for j in range(B_CONV):
            tap = rnd(cw_ref[j:j + 1, :]) * rnd(ext_ref[s, pl.ds(SUBLANES - (B_CONV - 1) + j, tb), :])
            conv = tap if conv is None else conv + tap
        conv = conv + cb_ref[...]
        ext_ref[s, 0:SUBLANES, :] = ext_ref[s, tb:tb + SUBLANES, :]
        xbc = _silu(conv)
        xs = xbc[:, 0:B_INNER]
        bm = xbc[:, B_INNER:B_INNER + B_GROUPS * B_STATE].astype(BF16)
        cm = xbc[:, B_INNER + B_GROUPS * B_STATE:].astype(BF16)

        dt = _softplus(cols_ref[s, :, B_INNER + B_CONV_DIM:] + dtb_ref[...])
        a = _exact_lhs_nn(tril, dt * neg_a)
        a_t = a.T
        dt_t = dt.T
        cb = [_nt(cm[:, g * B_STATE:(g + 1) * B_STATE], bm[:, g * B_STATE:(g + 1) * B_STATE])
              for g in range(B_GROUPS)]

        y_pairs = []
        for p in range(n_pair):
            grp = (2 * p) // B_HPG
            bg = bm[:, grp * B_STATE:(grp + 1) * B_STATE]
            cg = cm[:, grp * B_STATE:(grp + 1) * B_STATE]
            xp = xs[:, p * pair_w:(p + 1) * pair_w]
            sp = states[s][p]
            y_p = _nt(cg, sp.astype(BF16))
            a_cols, w_cols, e_last = [], [], []
            for r in range(2):
                hh = 2 * p + r
                a_col = a[:, hh:hh + 1]
                a_last = a[tb - 1:tb, hh:hh + 1]
                a_cols.append(a_col)
                w_cols.append(jnp.exp(a_last - a_col) * dt[:, hh:hh + 1])
                e_last.append(jnp.exp(a_last))
            y_p = y_p * jnp.where(lane_lo, jnp.exp(a_cols[0]), jnp.exp(a_cols[1]))
            for r in range(2):
                hh = 2 * p + r
                seg = jnp.where(causal, jnp.exp(jnp.where(causal, a_cols[r] - a_t[hh:hh + 1, :], 0.0)), 0.0)
                m = cb[grp] * seg * dt_t[hh:hh + 1, :]
                half = lane_lo if r == 0 else jnp.logical_not(lane_lo)
                y_p = y_p + _nn(m.astype(BF16), jnp.where(half, xp, 0.0).astype(BF16))
            xw = xp * jnp.where(lane_lo, w_cols[0], w_cols[1])
            new_states.append(sp * jnp.where(row_lo, e_last[0], e_last[1]) + _tn(xw.astype(BF16), bg))
            y_pairs.append(y_p)

        y = jnp.concatenate(y_pairs, axis=-1) + dsk_ref[...] * xs
        yz = y * _silu(z)
        o_ref[s] = (yz * lax.rsqrt(jnp.mean(yz * yz, axis=-1, keepdims=True) + RMS_EPS)
                    * ng_ref[...]).astype(o_ref.dtype)

    for s in range(nseq):
        for p in range(n_pair):
            st_ref[s, p] = new_states[s * n_pair + p]

    @pl.when(t == pl.num_programs(1) - 1)
    def _():
        sout_ref[...] = st_ref[...]


def _ssd(cols_b, nb, t_len, conv_w, conv_b, dt_bias, a_log, d_skip, norm_g, conv_state, ssm_state):
    tb = min(TIME_BLOCK, t_len)
    nt = t_len // tb
    n_pair = B_HEADS // 2
    pair_w = 2 * B_HEADDIM
    pad8 = lambda v: jnp.pad(v.reshape(1, B_HEADS), ((0, 0), (0, B_DT_PAD - B_HEADS)))
    const = lambda a: pl.BlockSpec(a.shape, lambda b, t: (0,) * a.ndim)
    args = (conv_w, conv_b.reshape(1, -1), pad8(dt_bias), pad8(a_log),
            jnp.repeat(d_skip, B_HEADDIM).reshape(1, B_INNER), norm_g.reshape(1, -1))
    nseq = _seqs_per_step(nb)
    state_spec = pl.BlockSpec((nseq, n_pair, pair_w, B_STATE), lambda b, t: (b, 0, 0, 0))
    o, s_out = pl.pallas_call(
        _ssd_kernel,
        grid=(nb // nseq, nt),
        in_specs=[pl.BlockSpec((nseq, tb, B_COLS_PAD), lambda b, t: (b, t, 0))]
                 + [const(a) for a in args]
                 + [pl.BlockSpec((nseq, B_CONV - 1, B_CONV_DIM), lambda b, t: (b, 0, 0)), state_spec],
        out_specs=[pl.BlockSpec((nseq, tb, B_INNER), lambda b, t: (b, t, 0)), state_spec],
        out_shape=[jax.ShapeDtypeStruct((nb, t_len, B_INNER), BF16),
                   jax.ShapeDtypeStruct((nb, n_pair, pair_w, B_STATE), F32)],
        scratch_shapes=[pltpu.VMEM((nseq, tb + SUBLANES, B_CONV_DIM), F32),
                        pltpu.VMEM((nseq, n_pair, pair_w, B_STATE), F32)],
        compiler_params=_params("arbitrary", "arbitrary"),
        name="ssd",
    )(cols_b.reshape(nb, t_len, B_COLS_PAD), *args, conv_state, ssm_state.reshape(nb, n_pair, pair_w, B_STATE))
    return o.reshape(nb * t_len, B_INNER), s_out.reshape(nb, B_HEADS, B_HEADDIM, B_STATE)


RWKV_PASSES = 3
RWKV_SEQS_PER_STEP = 4


def _rwkv_kernel(cols_ref, mu_ref, w0_ref, ww2_ref, a0_ref, aw2_ref, gw2_ref, kk_ref, ka_ref, rk_ref,
                 lnw_ref, lnb_ref, sh0_ref, s0_ref, o_ref, sout_ref, ext_ref, st_ref):
    t = pl.program_id(1)
    nseq, tb = cols_ref.shape[0], cols_ref.shape[1]
    n_pair = C_HEADS // 2
    pw = 2 * C_HEADDIM
    s3 = 3 * C_WIDTH
    mm = functools.partial(_mm, passes=RWKV_PASSES)

    @pl.when(t == 0)
    def _():
        ext_ref[:, 0:SUBLANES, :] = jnp.zeros((nseq, SUBLANES, C_COLS_PAD), F32)
        ext_ref[:, SUBLANES - 1:SUBLANES, :] = sh0_ref[...]
        st_ref[...] = s0_ref[...]

    head_of = lambda n, d: _iota2((n, n), d) >> (C_HEADDIM.bit_length() - 1)
    seg_ones = jnp.where(head_of(C_WIDTH, 0) == head_of(C_WIDTH, 1), 1.0, 0.0).astype(BF16)

    def segsum(x):
        hi, lo = _split2(x)
        return _nn(hi, seg_ones) + _nn(lo, seg_ones)

    tril = _tril_bf16(tb)
    rnd = lambda a: a.astype(BF16).astype(F32)

    def prep(s):
        cur = cols_ref[s]
        ext_ref[s, SUBLANES:SUBLANES + tb, :] = cur
        prev = ext_ref[s, pl.ds(SUBLANES - 1, tb), :]
        ext_ref[s, 0:SUBLANES, :] = ext_ref[s, tb:tb + SUBLANES, :]
        mixed = cur + (prev - cur) * mu_ref[...]
        r = mixed[:, 0:C_WIDTH]
        k = mixed[:, C_WIDTH:2 * C_WIDTH]
        v = mixed[:, 2 * C_WIDTH:s3]
        w_lo = mixed[:, s3:s3 + C_LORA_PAD]
        a_lo = mixed[:, s3 + C_LORA_PAD:s3 + 2 * C_LORA_PAD]
        g_lo = mixed[:, s3 + 2 * C_LORA_PAD:]
        w_raw = -_softplus(-(w0_ref[...] + _mm(_nn, jnp.tanh(w_lo), ww2_ref[...], 1))) - 0.5
        logw = -jnp.exp(w_raw)
        a = _sigmoid(a0_ref[...] + _mm(_nn, a_lo, aw2_ref[...], 1))
        g = _mm(_nn, _sigmoid(g_lo), gw2_ref[...], 1)
        kk = k * kk_ref[...]
        kk = kk / jnp.maximum(jnp.sqrt(segsum(kk * kk)), 1e-12)
        k2 = k * (1.0 + (a - 1.0) * ka_ref[...])
        beta = kk * a
        c = _exact_lhs_nn(tril, logw)
        c_last = c[tb - 1:tb, :]
        g_inv = jnp.exp(-c)
        g_hat = jnp.exp(c_last - c)
        return dict(r=r, v=v, g=g, k2=k2, gam_last=jnp.exp(c_last),
                    ab=-rnd(kk) * jnp.exp(c - logw), rb=rnd(r) * jnp.exp(c),
                    bt=beta * g_inv, kt=k2 * g_inv, bh=beta * g_hat, kh=k2 * g_hat)

    seq = [prep(s) for s in range(nseq)]

    n2 = 2 * tb
    ri = _iota2((n2, n2), 0)
    ci = _iota2((n2, n2), 1)
    keep = jnp.where(ri < tb, ci & (tb - 1), (ci & (tb - 1)) - 1) < (ri & (tb - 1))
    eye = jnp.where(ri == ci, 1.0, 0.0)
    left_h = _iota2((tb, n2), 1) < tb
    lane0 = _iota2((1, pw), 1) < C_HEADDIM
    rows_top = _iota2((n2, 1), 0) < tb
    d_own = (rows_top & lane0) | (jnp.logical_not(rows_top) & jnp.logical_not(lane0))
    blockdiag = (_iota2((pw, pw), 0) < C_HEADDIM) == (_iota2((pw, pw), 1) < C_HEADDIM)
    stack = lambda a, b: jnp.concatenate([a, b], axis=0)
    unit = [(s, p) for s in range(nseq) for p in range(n_pair)]
    pairs = range(len(unit))
    col = lambda name: [seq[s][name][:, p * pw:(p + 1) * pw] for s, p in unit]
    sp = [st_ref[s, p] for s, p in unit]
    vp, ab, rb, bt, kt = col('v'), col('ab'), col('rb'), col('bt'), col('kt')
    x = [stack(ab[p], rb[p]) for p in pairs]
    yk = [jnp.concatenate([jnp.where(lane0, bt[p], 0.0), jnp.where(lane0, kt[p], 0.0),
                           jnp.where(lane0, 0.0, kt[p]), jnp.where(lane0, 0.0, bt[p])], axis=0)
          for p in pairs]
    pp = [mm(_nt, x[p], yk[p]) for p in pairs]
    blk0 = [jnp.where(keep, pp[p][:, 0:n2], 0.0) for p in pairs]
    blk1 = [jnp.where(keep, pp[p][:, n2:2 * n2], 0.0) for p in pairs]
    a_bd = [stack(jnp.where(left_h, blk0[p][0:tb], 0.0), jnp.where(left_h, 0.0, blk1[p][0:tb])) for p in pairs]
    q_ak = [stack(jnp.where(left_h, 0.0, blk0[p][0:tb]), jnp.where(left_h, blk1[p][0:tb], 0.0)) for p in pairs]
    r_mat = [stack(blk1[p][tb:n2], blk0[p][tb:n2]) for p in pairs]
    gs = [mm(_nt, x[p], sp[p]) for p in pairs]
    w = [jnp.where(d_own, stack(gs[p][0:tb], gs[p][0:tb]) + mm(_nn, q_ak[p], stack(vp[p], vp[p])), 0.0)
         for p in pairs]
    t_inv = [eye + a_bd[p] for p in pairs]
    a_pow = list(a_bd)
    for _ in range(max(1, (tb - 1).bit_length()) - 1):
        a_pow = [_mm(_nn, a_pow[p], a_pow[p], 1) for p in pairs]
        t_inv = [t_inv[p] + _mm(_nn, t_inv[p], a_pow[p], 1) for p in pairs]
    u = [_mm(_nn, t_inv[p], w[p], 1) for p in pairs]
    res = [w[p] - u[p] + mm(_nn, a_bd[p], u[p]) for p in pairs]
    u = [u[p] + _mm(_nn, t_inv[p], res[p], 1) for p in pairs]
    zz = [u[p] + stack(jnp.where(lane0, 0.0, vp[p]), jnp.where(lane0, vp[p], 0.0)) for p in pairs]
    yf = [jnp.where(d_own, 0.0, mm(_nn, r_mat[p], zz[p]) + stack(gs[p][tb:n2], gs[p][tb:n2])) for p in pairs]
    y_pairs = [yf[p][0:tb] + yf[p][tb:n2] for p in pairs]
    uv = [stack(u[p][0:tb] + u[p][tb:n2], vp[p]) for p in pairs]
    bh, kh, gam = col('bh'), col('kh'), col('gam_last')
    s_new = [sp[p] * gam[p] + jnp.where(blockdiag, mm(_tn, uv[p], stack(bh[p], kh[p])), 0.0) for p in pairs]
    for i, (s, p) in enumerate(unit):
        st_ref[s, p] = s_new[i]

    inv_n = 1.0 / C_HEADDIM
    for s in range(nseq):
        d = seq[s]
        y = jnp.concatenate(y_pairs[s * n_pair:(s + 1) * n_pair], axis=-1)
        mean = segsum(y) * inv_n
        dlt = y - mean
        var = segsum(dlt * dlt) * inv_n
        yn = dlt * lax.rsqrt(var + C_LN_EPS) * lnw_ref[...] + lnb_ref[...]
        bonus = segsum(d['r'] * d['k2'] * rk_ref[...]) * d['v']
        o_ref[s] = ((yn + bonus) * d['g']).astype(o_ref.dtype)

    @pl.when(t == pl.num_programs(1) - 1)
    def _():
        sout_ref[...] = st_ref[...]


def _pad_lora_cols(a):
    s3 = 3 * C_WIDTH
    z = jnp.zeros(a.shape[:-1] + (C_LORA_PAD - C_DECAY_LORA,), a.dtype)
    return jnp.concatenate([a[..., :s3 + C_DECAY_LORA], z,
                            a[..., s3 + C_DECAY_LORA:s3 + C_DECAY_LORA + C_ICLR_LORA], z,
                            a[..., s3 + C_DECAY_LORA + C_ICLR_LORA:]], axis=-1)


def _unpad_lora_cols(a):
    s3 = 3 * C_WIDTH
    return jnp.concatenate([a[..., :s3 + C_DECAY_LORA],
                            a[..., s3 + C_LORA_PAD:s3 + C_LORA_PAD + C_ICLR_LORA],
                            a[..., s3 + 2 * C_LORA_PAD:]], axis=-1)


def _rwkv(cols_c, nb, t_len, shift_state, wkv_state, mu, w0, w_w2, a0, a_w2, g_w2, k_k, k_a, r_k, ln_w, ln_b):
    tb = min(TIME_BLOCK, t_len)
    nt = t_len // tb
    n_pair = C_HEADS // 2
    pw = 2 * C_HEADDIM
    row = lambda v: v.reshape(1, -1)
    pad_rows = lambda w: jnp.pad(w, ((0, C_LORA_PAD - w.shape[0]), (0, 0)))
    args = (row(_pad_lora_cols(mu)), row(w0), pad_rows(w_w2), row(a0), pad_rows(a_w2), g_w2,
            row(k_k), row(k_a), row(r_k), row(ln_w), row(ln_b))
    const = lambda a: pl.BlockSpec(a.shape, lambda b, t: (0,) * a.ndim)
    s4 = wkv_state.reshape(nb, n_pair, 2, C_HEADDIM, C_HEADDIM)
    zero = jnp.zeros_like(s4[:, :, 0])
    s_bd = jnp.concatenate([jnp.concatenate([s4[:, :, 0], zero], axis=-1),
                            jnp.concatenate([zero, s4[:, :, 1]], axis=-1)], axis=-2)
    nseq = _seqs_per_step(nb, RWKV_SEQS_PER_STEP)
    state_spec = pl.BlockSpec((nseq, n_pair, pw, pw), lambda b, t: (b, 0, 0, 0))
    o, s_out = pl.pallas_call(
        _rwkv_kernel,
        grid=(nb // nseq, nt),
        in_specs=[pl.BlockSpec((nseq, tb, C_COLS_PAD), lambda b, t: (b, t, 0))]
                 + [const(a) for a in args]
                 + [pl.BlockSpec((nseq, 1, C_COLS_PAD), lambda b, t: (b, 0, 0)), state_spec],
        out_specs=[pl.BlockSpec((nseq, tb, C_WIDTH), lambda b, t: (b, t, 0)), state_spec],
        out_shape=[jax.ShapeDtypeStruct((nb, t_len, C_WIDTH), BF16),
                   jax.ShapeDtypeStruct((nb, n_pair, pw, pw), F32)],
        scratch_shapes=[pltpu.VMEM((nseq, tb + SUBLANES, C_COLS_PAD), F32),
                        pltpu.VMEM((nseq, n_pair, pw, pw), F32)],
        compiler_params=_params("arbitrary", "arbitrary"),
        name="rwkv7",
    )(cols_c.reshape(nb, t_len, C_COLS_PAD), *args, _pad_lora_cols(shift_state), s_bd)
    s_new = jnp.stack([s_out[:, :, :C_HEADDIM, :C_HEADDIM], s_out[:, :, C_HEADDIM:, C_HEADDIM:]], axis=2)
    return o.reshape(nb * t_len, C_WIDTH), s_new.reshape(nb, C_HEADS, C_HEADDIM, C_HEADDIM)


def _merge_kernel(seqs, x_ref, g_ref, sh_ref, sc_ref, gt_ref, oa_ref, ob_ref, oc_ref,
                  wg_ref, wb_ref, wo_ref, out_ref):
    x = x_ref[...]
    rows = x.shape[0]
    h = _norm_mod(x, g_ref[...], sh_ref[...], sc_ref[...], seqs).astype(BF16)
    merged = jnp.zeros((rows, D_MODEL), F32)
    off = 0
    for i, o_ref in enumerate((oa_ref, ob_ref, oc_ref)):
        width = o_ref.shape[1]
        gate = _sigmoid(_nn(h, wg_ref[:, i * D_MODEL:(i + 1) * D_MODEL]))
        merged = merged + gate * _nn(o_ref[...], wb_ref[off:off + width, :])
        off += width
    mix = _nn(merged.astype(BF16), wo_ref[...])
    out = x.reshape(seqs, rows // seqs, D_MODEL) + gt_ref[...] * mix.reshape(seqs, rows // seqs, D_MODEL)
    out_ref[...] = out.reshape(rows, D_MODEL)


def _merge(x, t_len, g, ada3, o_a, o_b, o_c, wg, wb, wo, tile):
    n = x.shape[0]
    seqs, tps = _tile_plan(t_len, tile)
    full = lambda a: pl.BlockSpec(a.shape, lambda i: (0,) * a.ndim)
    ada_spec = lambda j: pl.BlockSpec((seqs, 1, D_MODEL), lambda i: (i // tps, 0, j))
    row = lambda w: pl.BlockSpec((tile, w), lambda i: (i, 0))
    return pl.pallas_call(
        functools.partial(_merge_kernel, seqs),
        grid=(n // tile,),
        in_specs=[row(D_MODEL), full(g), ada_spec(0), ada_spec(1), ada_spec(2),
                  row(A_WIDTH), row(B_INNER), row(C_WIDTH), full(wg), full(wb), full(wo)],
        out_specs=row(D_MODEL),
        out_shape=jax.ShapeDtypeStruct((n, D_MODEL), F32),
        compiler_params=_params("arbitrary"),
        name="merge",
    )(x, g, ada3, ada3, ada3, o_a, o_b, o_c, wg, wb, wo)


def _route(scores, bias):
    lane = _iota2(scores.shape, 1)
    sel = scores + bias

    def partner(x, d, span):
        wrap = (lane & (span - 1)) + d >= span
        fwd = pltpu.roll(x, x.shape[1] - d, 1)
        back = pltpu.roll(x, span - d, 1)
        return jnp.where(wrap, back, fwd), wrap

    rank = jnp.zeros(scores.shape, jnp.int32)
    for d in range(1, EXPERTS_PER_GROUP):
        other, wrap = partner(sel, d, EXPERTS_PER_GROUP)
        beats = (other > sel) | (wrap & (other == sel))
        rank = rank + jnp.where(beats, 1, 0)
    in_top = rank < 2
    g_score = jnp.where(in_top, sel, 0.0)
    top_s = jnp.where(in_top, scores, 0.0)
    g_sum, w_sum = g_score, top_s
    for d in range(1, EXPERTS_PER_GROUP):
        g_sum = g_sum + partner(g_score, d, EXPERTS_PER_GROUP)[0]
        w_sum = w_sum + partner(top_s, d, EXPERTS_PER_GROUP)[0]
    chosen = in_top & (lane < N_EXPERTS)
    for d in range(EXPERTS_PER_GROUP, N_EXPERTS, EXPERTS_PER_GROUP):
        other, wrap = partner(g_sum, d, N_EXPERTS)
        chosen = chosen & ((other < g_sum) | (jnp.logical_not(wrap) & (other == g_sum)))
    return jnp.where(chosen, scores / w_sum, 0.0), chosen


MOE_ROW_BLOCK = 128
MOE_EXPERTS_PER_STEP = 4


def _expert_out(xb, c_e, wg, wu, wd):
    hid = _silu(_nn(xb, wg)) * _nn(xb, wu)
    y = _nn(hid.astype(BF16), wd)
    return jnp.where(c_e != 0.0, c_e * y, 0.0)


def _lane_pick(a, lane_idx):
    return jnp.sum(jnp.where(_iota2(a.shape, 1) == lane_idx, a, 0.0), axis=-1, keepdims=True)


def _residual_out(seqs, final, x, moe, gate, final_g):
    rows = x.shape[0]
    out = x.reshape(seqs, rows // seqs, D_MODEL) + gate * moe.reshape(seqs, rows // seqs, D_MODEL)
    out = out.reshape(rows, D_MODEL)
    if final:
        out = out * lax.rsqrt(jnp.mean(out * out, axis=-1, keepdims=True) + RMS_EPS) * final_g
    return out


def _moe_kernel(seqs, final, x_ref, g_ref, sh_ref, sc_ref, gt_ref, wr_ref, br_ref, fg_ref,
                wg_ref, wu_ref, wd_ref, out_ref, xs_ref, cc_ref, dst_ref, acc_ref, seg_ref):
    step = pl.program_id(1)
    per_step = wg_ref.shape[0]
    rows = x_ref.shape[0]
    ncomp = xs_ref.shape[0]
    lanes = cc_ref.shape[1]
    rb = MOE_ROW_BLOCK

    @pl.when(step == 0)
    def _():
        h = _norm_mod(x_ref[...], g_ref[...], sh_ref[...], sc_ref[...], seqs)
        scores = _sigmoid(_mm(_nn, h, wr_ref[...], 1))
        coef, chosen = _route(scores, br_ref[...])
        shift = EXPERTS_PER_GROUP.bit_length() - 1
        li, lj = _iota2((lanes, lanes), 0), _iota2((lanes, lanes), 1)
        gmat = jnp.where(((li >> shift) == lj) & (li < N_EXPERTS), 0.5, 0.0).astype(BF16)
        member = _nn(jnp.where(chosen, 1.0, 0.0).astype(BF16), gmat)
        before = jnp.where(_iota2((rows, rows), 1) < _iota2((rows, rows), 0), 1.0, 0.0).astype(BF16)
        rank = _nn(before, member.astype(BF16))
        count = rank[rows - 1:rows, :] + member[rows - 1:rows, :]
        blocks = jnp.floor((count + (rb - 1)) * (1.0 / rb))
        earlier = jnp.where(li < lj, 1.0, 0.0).astype(BF16)
        start = _nn(jnp.broadcast_to(blocks * rb, (SUBLANES, lanes)).astype(BF16), earlier)[0:1, :]
        lane_row = _iota2((1, lanes), 1)
        for gi in range(N_EXPERT_GROUPS):
            seg_ref[gi] = jnp.sum(jnp.where(lane_row == gi, start, 0.0)).astype(jnp.int32)
            seg_ref[N_EXPERT_GROUPS + gi] = jnp.sum(jnp.where(lane_row == gi, blocks, 0.0)).astype(jnp.int32)
        dest = jnp.sum(member * (start + rank), axis=-1, keepdims=True)
        dest_row = jnp.broadcast_to(dest, (rows, lanes)).T[0:1, :].astype(jnp.int32)
        perm = jnp.where(_iota2((ncomp, rows), 0) == dest_row, 1.0, 0.0).astype(BF16)
        xs_ref[...] = _nn(perm, h.astype(BF16)).astype(BF16)
        c_hi, c_lo = _split2(coef)
        cc_ref[...] = _nn(perm, c_hi) + _nn(perm, c_lo)
        dst_ref[...] = jnp.broadcast_to(dest, (rows, lanes))
        acc_ref[...] = jnp.zeros_like(acc_ref)

    grp = (step * per_step) // EXPERTS_PER_GROUP

    def block(k, carry):
        r0 = pl.multiple_of(seg_ref[grp] + k * rb, rb)
        xb = xs_ref[pl.ds(r0, rb), :]
        cc = cc_ref[pl.ds(r0, rb), :]
        upd = None
        for j in range(per_step):
            y = _expert_out(xb, _lane_pick(cc, step * per_step + j), wg_ref[j], wu_ref[j], wd_ref[j])
            upd = y if upd is None else upd + y
        acc_ref[pl.ds(r0, rb), :] += upd
        return carry

    lax.fori_loop(0, seg_ref[N_EXPERT_GROUPS + grp], block, 0)

    @pl.when(step == pl.num_programs(1) - 1)
    def _():
        a_hi, a_lo = _split2(acc_ref[...])
        dest_i = dst_ref[:, 0:1].astype(jnp.int32)
        unperm = jnp.where(_iota2((rows, ncomp), 1) == dest_i, 1.0, 0.0).astype(BF16)
        moe = _nn(unperm, a_hi) + _nn(unperm, a_lo)
        out_ref[...] = _residual_out(seqs, final, x_ref[...], moe, gt_ref[...], fg_ref[...])


def _moe(x, t_len, g, ada3, w_router, b_router, final_g, wg, wu, wd, tile, final):
    n = x.shape[0]
    seqs, tps = _tile_plan(t_len, tile)
    ncomp = tile + N_EXPERT_GROUPS * MOE_ROW_BLOCK
    full = lambda a: pl.BlockSpec(a.shape, lambda i, e: (0,) * a.ndim)
    ada_spec = lambda j: pl.BlockSpec((seqs, 1, D_MODEL), lambda i, e: (i // tps, 0, j))
    row = pl.BlockSpec((tile, D_MODEL), lambda i, e: (i, 0))
    assert EXPERTS_PER_GROUP % MOE_EXPERTS_PER_STEP == 0
    expert = lambda a: pl.BlockSpec((MOE_EXPERTS_PER_STEP,) + a.shape[1:], lambda i, e: (e, 0, 0))
    return pl.pallas_call(
        functools.partial(_moe_kernel, seqs, final),
        grid=(n // tile, N_EXPERTS // MOE_EXPERTS_PER_STEP),
        in_specs=[pl.BlockSpec((tile, D_MODEL), lambda i, e: (i, 0), pipeline_mode=pl.Buffered(1)),
                  full(g), ada_spec(3), ada_spec(4), ada_spec(5), full(w_router), full(b_router),
                  full(final_g), expert(wg), expert(wu), expert(wd)],
        out_specs=pl.BlockSpec((tile, D_MODEL), lambda i, e: (i, 0), pipeline_mode=pl.Buffered(1)),
        out_shape=jax.ShapeDtypeStruct((n, D_MODEL), F32),
        scratch_shapes=[pltpu.VMEM((ncomp, D_MODEL), BF16),
                        pltpu.VMEM((ncomp, LANES), F32),
                        pltpu.VMEM((tile, LANES), F32),
                        pltpu.VMEM((ncomp, D_MODEL), F32),
                        pltpu.SMEM((2 * N_EXPERT_GROUPS,), jnp.int32)],
        compiler_params=_params("arbitrary", "arbitrary"),
        name="moe",
    )(x, g, ada3, ada3, ada3, w_router, b_router, final_g, wg, wu, wd)


def _prep_weights(w_in):
    o_b = A_COLS
    o_c = A_COLS + B_COLS
    o_g = o_c + C_COLS
    wa = w_in[..., :o_b]
    wb = jnp.pad(w_in[..., o_b:o_c], ((0, 0), (0, 0), (0, B_DT_PAD - B_HEADS)))
    wc = _pad_lora_cols(w_in[..., o_c:o_g])
    wgate = w_in[..., o_g:]
    return tuple(w.astype(BF16) for w in (wa, wb, wc, wgate))


def _trunk(x, ada, states, weights, tiles):
    nb, t_len, _ = x.shape
    st_hgrn, st_ssm, st_conv, st_wkv, st_shift = states
    (wa, wb, wc, wgate, w_branch, w_out, w_gate_e, w_up_e, w_down_e, w_router, b_router, p) = weights
    tile_tok, tile_moe = tiles
    depth = wa.shape[0]
    xf = x.reshape(nb * t_len, D_MODEL)
    row = lambda v: v.reshape(1, -1)
    outs = []
    for l in range(depth):
        ada3 = ada[l].reshape(nb, 1, -1)
        g1 = row(p['norm1_g'][l])
        cols_a, cols_b, cols_c = _inproj(xf, t_len, g1, ada3, wa[l], wb[l], wc[l], tile_tok)
        o_a, hgrn_new = _hgrn(cols_a, nb, t_len, l, p['hgrn_lb'], row(p['hgrn_norm_g'][l]), st_hgrn[l])
        o_b, ssm_new = _ssd(cols_b, nb, t_len, p['ssm_conv_w'][l], p['ssm_conv_b'][l], p['ssm_dt_bias'][l],
                            p['ssm_a_log'][l], p['ssm_d'][l], p['ssm_norm_g'][l], st_conv[l], st_ssm[l])
        o_c, wkv_new = _rwkv(cols_c, nb, t_len, st_shift[l], st_wkv[l], p['rwkv_mu'][l], p['rwkv_w0'][l],
                             p['rwkv_w_w2'][l], p['rwkv_a0'][l], p['rwkv_a_w2'][l], p['rwkv_g_w2'][l],
                             p['rwkv_k_k'][l], p['rwkv_k_a'][l], p['rwkv_r_k'][l].reshape(-1),
                             p['rwkv_ln_w'][l], p['rwkv_ln_b'][l])
        assert t_len >= B_CONV - 1
        conv_new = cols_b.reshape(nb, t_len, B_COLS_PAD)[:, t_len - (B_CONV - 1):, B_INNER:B_INNER + B_CONV_DIM]
        shift_new = _unpad_lora_cols(cols_c.reshape(nb, t_len, C_COLS_PAD)[:, t_len - 1:])
        xf = _merge(xf, t_len, g1, ada3, o_a, o_b, o_c, wgate[l], w_branch[l], w_out[l], tile_tok)
        xf = _moe(xf, t_len, row(p['norm2_g'][l]), ada3, w_router, b_router, row(p['final_g']),
                  w_gate_e[l], w_up_e[l], w_down_e[l], tile_moe, final=(l == depth - 1))
        outs.append((hgrn_new, ssm_new, conv_new, wkv_new, shift_new))
    new_states = [jnp.stack(s, axis=0) for s in zip(*outs)]
    return xf.reshape(nb, t_len, D_MODEL), new_states


def kernel(x_prompt, x_sample, c_prompt, c_sample, state_hgrn, state_ssm, state_conv, state_wkv, state_shift,
           w_ada, b_ada, norm1_g, norm2_g, final_g, w_in, hgrn_lb, hgrn_norm_g,
           ssm_conv_w, ssm_conv_b, ssm_dt_bias, ssm_a_log, ssm_d, ssm_norm_g,
           rwkv_mu, rwkv_w0, rwkv_w_w2, rwkv_a0, rwkv_a_w2, rwkv_g_w2, rwkv_k_k, rwkv_k_a, rwkv_r_k,
           rwkv_ln_w, rwkv_ln_b, w_branch, w_out, w_router, b_router, w_gate_e, w_up_e, w_down_e):
    p = {'norm1_g': norm1_g, 'norm2_g': norm2_g, 'final_g': final_g, 'hgrn_lb': hgrn_lb,
         'hgrn_norm_g': hgrn_norm_g, 'ssm_conv_w': ssm_conv_w, 'ssm_conv_b': ssm_conv_b,
         'ssm_dt_bias': ssm_dt_bias, 'ssm_a_log': ssm_a_log, 'ssm_d': ssm_d, 'ssm_norm_g': ssm_norm_g,
         'rwkv_mu': rwkv_mu, 'rwkv_w0': rwkv_w0, 'rwkv_w_w2': rwkv_w_w2, 'rwkv_a0': rwkv_a0,
         'rwkv_a_w2': rwkv_a_w2, 'rwkv_g_w2': rwkv_g_w2, 'rwkv_k_k': rwkv_k_k, 'rwkv_k_a': rwkv_k_a,
         'rwkv_r_k': rwkv_r_k, 'rwkv_ln_w': rwkv_ln_w, 'rwkv_ln_b': rwkv_ln_b}
    depth = w_in.shape[0]
    bp, tp = x_prompt.shape[0], x_prompt.shape[1]
    bs, ts = x_sample.shape[0], x_sample.shape[1]
    wa, wb, wc, wgate = _prep_weights(w_in)
    w_router_pad = jnp.pad(w_router, ((0, 0), (0, LANES - N_EXPERTS)))
    b_router_pad = jnp.pad(b_router.reshape(1, -1), ((0, 0), (0, LANES - N_EXPERTS)))
    weights = (wa, wb, wc, wgate, w_branch.astype(BF16), w_out.astype(BF16),
               w_gate_e.astype(BF16), w_up_e.astype(BF16), w_down_e.astype(BF16),
               w_router_pad, b_router_pad, p)
    ada = _ada(jnp.concatenate([c_prompt, c_sample], axis=0), w_ada, b_ada)

    zeros_p = (jnp.zeros((depth, bp, A_HEADS, A_DK, A_DV), F32),
               jnp.zeros((depth, bp, B_HEADS, B_HEADDIM, B_STATE), F32),
               jnp.zeros((depth, bp, B_CONV - 1, B_CONV_DIM), x_prompt.dtype),
               jnp.zeros((depth, bp, C_HEADS, C_HEADDIM, C_HEADDIM), F32),
               jnp.zeros((depth, bp, 1, C_COLS), x_prompt.dtype))
    tile_p = min(256, tp)
    y_prompt, st_p = _trunk(x_prompt, ada[:, :bp], zeros_p, weights, (tile_p, min(1024, tp)))
    tile_s = min(256, bs * ts)
    y_sample, st_s = _trunk(x_sample, ada[:, bp:], (state_hgrn, state_ssm, state_conv, state_wkv, state_shift),
                            weights, (tile_s, min(512, bs * ts)))
    return (y_prompt, y_sample, *st_p, *st_s)
```
